```python
import math
import jax, jax.numpy as jnp
from jax import lax
import numpy as np

D_MODEL = 1024
BATCH = 16
SEQ = 2048
DEPTH = 4

N_MIXERS = 2
ATTN_HEADS = 8
ATTN_HEAD_DIM = 64
ATTN_WIDTH = ATTN_HEADS * 2 * ATTN_HEAD_DIM
Q_BLOCK = 128
LRU_WIDTH = D_MODEL
LRU_BLOCKS = 4
LRU_BLOCK_DIM = LRU_WIDTH // LRU_BLOCKS
CONV_WIDTH = 4
LRU_C = 8.0
NORM_EPS = 1e-6
N_ATTN_LAYERS = (DEPTH + N_MIXERS - 1) // N_MIXERS
N_LRU_LAYERS = DEPTH // N_MIXERS

kernel_name = "hybrid_diffattn_rglru_sandwich"


def rms_norm(x, w):
    xf = x.astype(jnp.float32)
    y = xf * lax.rsqrt(jnp.mean(xf * xf, axis=-1, keepdims=True) + NORM_EPS)
    return (y * w.astype(jnp.float32)).astype(x.dtype)


def alibi_slopes(n_heads):
    h = jnp.arange(1, n_heads + 1, dtype=jnp.float32)
    return jnp.exp2(-(8.0 / n_heads) * h)


def diff_attention_core(q, k, v, lam):
    B, H, _, S, d = q.shape
    n_blk = S // Q_BLOCK
    scale = d ** -0.5
    slopes = alibi_slopes(H)
    q_blocks = q.reshape(B, H, 2, n_blk, Q_BLOCK, d).transpose(3, 0, 1, 2, 4, 5)
    key_pos = jnp.arange(S)

    def one_block(args):
        q_blk, blk_idx = args
        scores = jnp.einsum('bhmqd,bhmkd->bhmqk', q_blk, k).astype(jnp.float32) * scale
        query_pos = blk_idx * Q_BLOCK + jnp.arange(Q_BLOCK)
        dist = (query_pos[:, None] - key_pos[None, :]).astype(jnp.float32)
        bias = -slopes[:, None, None, None] * dist[None, None]
        scores = jnp.where(dist >= 0, scores + bias, -jnp.inf)
        probs = jax.nn.softmax(scores, axis=-1)
        weights = probs[:, :, 0] - lam * probs[:, :, 1]
        return jnp.einsum('bhqk,bhke->bhqe', weights.astype(v.dtype), v)

    out = lax.map(one_block, (q_blocks, jnp.arange(n_blk)))
    return out.transpose(1, 0, 3, 2, 4).reshape(B, S, H, 2 * d)


def diff_attention_mixer(h, w_in, w_out, lq1, lk1, lq2, lk2, subln_w, lam_init):
    B, S, _ = h.shape
    H, d = ATTN_HEADS, ATTN_HEAD_DIM
    proj = h @ w_in
    q, k, v, gate = jnp.split(proj, 4, axis=-1)
    q = q.reshape(B, S, H, 2, d).transpose(0, 2, 3, 1, 4)
    k = k.reshape(B, S, H, 2, d).transpose(0, 2, 3, 1, 4)
    v = v.reshape(B, S, H, 2 * d).transpose(0, 2, 1, 3)
    lam = (jnp.exp(jnp.sum(lq1.astype(jnp.float32) * lk1.astype(jnp.float32)))
           - jnp.exp(jnp.sum(lq2.astype(jnp.float32) * lk2.astype(jnp.float32)))
           + lam_init)
    o = diff_attention_core(q, k, v, lam)
    o = rms_norm(o, subln_w) * (1.0 - lam_init)
    o = o.reshape(B, S, ATTN_WIDTH)
    return (o * jax.nn.silu(gate)) @ w_out


def causal_depthwise_conv(x, w, b):
    C = x.shape[-1]
    y = lax.conv_general_dilated(
        x, w[:, None, :].astype(x.dtype), window_strides=(1,),
        padding=[(CONV_WIDTH - 1, 0)], dimension_numbers=('NWC', 'WIO', 'NWC'),
        feature_group_count=C)
    return y + b


def block_diag_linear(x, w, b):
    B, S, _ = x.shape
    xb = x.reshape(B, S, LRU_BLOCKS, LRU_BLOCK_DIM)
    y = jnp.einsum('bsgi,gij->bsgj', xb, w).reshape(B, S, LRU_WIDTH)
    return y + b


def rg_lru(x, gate_a_w, gate_a_b, gate_x_w, gate_x_b, log_a_param):
    r = jax.nn.sigmoid(block_diag_linear(x, gate_a_w, gate_a_b).astype(jnp.float32))
    i = jax.nn.sigmoid(block_diag_linear(x, gate_x_w, gate_x_b).astype(jnp.float32))
    log_a = -LRU_C * r * jax.nn.softplus(-log_a_param.astype(jnp.float32))
    a = jnp.exp(log_a)
    mult = jnp.sqrt(-jnp.expm1(2.0 * log_a))
    b = mult * (i * x.astype(jnp.float32))

    def combine(left, right):
        a_l, b_l = left
        a_r, b_r = right
        return a_l * a_r, a_r * b_l + b_r

    _, h = lax.associative_scan(combine, (a, b), axis=1)
    return h.astype(x.dtype)


def rglru_mixer(h, w_in, conv_w, conv_b, gate_a_w, gate_a_b, gate_x_w, gate_x_b,
                log_a_param, w_out):
    proj = h @ w_in
    xr, gate = jnp.split(proj, 2, axis=-1)
    xr = causal_depthwise_conv(xr, conv_w, conv_b)
    y = rg_lru(xr, gate_a_w, gate_a_b, gate_x_w, gate_x_b, log_a_param)
    return (y * jax.nn.silu(gate)) @ w_out


def setup_inputs(seed: int = 0) -> dict:
    key = jax.random.key(seed)
    ks = jax.random.split(key, 20)
    f32 = jnp.float32
    D, NA, NL = D_MODEL, N_ATTN_LAYERS, N_LRU_LAYERS
    x = jax.random.normal(ks[0], (BATCH, SEQ, D), f32)
    pre_norm_w = 1.0 + 0.05 * jax.random.normal(ks[1], (DEPTH, D), f32)
    post_norm_w = 1.0 + 0.05 * jax.random.normal(ks[2], (DEPTH, D), f32)
    attn_w_in = jax.random.normal(ks[3], (NA, D, 4 * ATTN_WIDTH), f32) * D ** -0.5
    attn_w_out = jax.random.normal(ks[4], (NA, ATTN_WIDTH, D), f32) * ATTN_WIDTH ** -0.5
    attn_lambda_q1 = 0.1 * jax.random.normal(ks[5], (NA, ATTN_HEAD_DIM), f32)
    attn_lambda_k1 = 0.1 * jax.random.normal(ks[6], (NA, ATTN_HEAD_DIM), f32)
    attn_lambda_q2 = 0.1 * jax.random.normal(ks[7], (NA, ATTN_HEAD_DIM), f32)
    attn_lambda_k2 = 0.1 * jax.random.normal(ks[8], (NA, ATTN_HEAD_DIM), f32)
    attn_subln_w = 1.0 + 0.05 * jax.random.normal(ks[9], (NA, 2 * ATTN_HEAD_DIM), f32)
    lru_w_in = jax.random.normal(ks[10], (NL, D, 2 * LRU_WIDTH), f32) * D ** -0.5
    lru_conv_w = jax.random.normal(ks[11], (NL, CONV_WIDTH, LRU_WIDTH), f32) * CONV_WIDTH ** -0.5
    lru_conv_b = 0.01 * jax.random.normal(ks[12], (NL, LRU_WIDTH), f32)
    lru_gate_a_w = jax.random.normal(ks[13], (NL, LRU_BLOCKS, LRU_BLOCK_DIM, LRU_BLOCK_DIM), f32) * LRU_BLOCK_DIM ** -0.5
    lru_gate_a_b = 0.01 * jax.random.normal(ks[14], (NL, LRU_WIDTH), f32)
    lru_gate_x_w = jax.random.normal(ks[15], (NL, LRU_BLOCKS, LRU_BLOCK_DIM, LRU_BLOCK_DIM), f32) * LRU_BLOCK_DIM ** -0.5
    lru_gate_x_b = 0.01 * jax.random.normal(ks[16], (NL, LRU_WIDTH), f32)
    a_c = jax.random.uniform(ks[17], (NL, LRU_WIDTH), f32, 0.9, 0.999)
    s = a_c ** (1.0 / LRU_C)
    lru_log_a_param = jnp.log(s) - jnp.log1p(-s)
    lru_w_out = jax.random.normal(ks[18], (NL, LRU_WIDTH, D), f32) * LRU_WIDTH ** -0.5
    return {
        "x": x, "pre_norm_w": pre_norm_w, "post_norm_w": post_norm_w,
        "attn_w_in": attn_w_in, "attn_w_out": attn_w_out,
        "attn_lambda_q1": attn_lambda_q1, "attn_lambda_k1": attn_lambda_k1,
        "attn_lambda_q2": attn_lambda_q2, "attn_lambda_k2": attn_lambda_k2,
        "attn_subln_w": attn_subln_w,
        "lru_w_in": lru_w_in, "lru_conv_w": lru_conv_w, "lru_conv_b": lru_conv_b,
        "lru_gate_a_w": lru_gate_a_w, "lru_gate_a_b": lru_gate_a_b,
        "lru_gate_x_w": lru_gate_x_w, "lru_gate_x_b": lru_gate_x_b,
        "lru_log_a_param": lru_log_a_param, "lru_w_out": lru_w_out,
    }


def reference(x, pre_norm_w, post_norm_w, attn_w_in, attn_w_out, attn_lambda_q1,
              attn_lambda_k1, attn_lambda_q2, attn_lambda_k2, attn_subln_w,
              lru_w_in, lru_conv_w, lru_conv_b, lru_gate_a_w, lru_gate_a_b,
              lru_gate_x_w, lru_gate_x_b, lru_log_a_param, lru_w_out):
    for layer in range(DEPTH):
        h = rms_norm(x, pre_norm_w[layer])
        j = layer // N_MIXERS
        if layer % N_MIXERS == 0:
            lam_init = 0.8 - 0.6 * math.exp(-0.3 * layer)
            out = diff_attention_mixer(
                h, attn_w_in[j], attn_w_out[j], attn_lambda_q1[j], attn_lambda_k1[j],
                attn_lambda_q2[j], attn_lambda_k2[j], attn_subln_w[j], lam_init)
        else:
            out = rglru_mixer(
                h, lru_w_in[j], lru_conv_w[j], lru_conv_b[j], lru_gate_a_w[j],
                lru_gate_a_b[j], lru_gate_x_w[j], lru_gate_x_b[j],
                lru_log_a_param[j], lru_w_out[j])
        x = x + rms_norm(out, post_norm_w[layer])
    return x
```

```python
import functools
import math

import jax
import jax.numpy as jnp
from jax import lax
from jax.experimental import pallas as pl
from jax.experimental.pallas import tpu as pltpu

F32 = jnp.float32
BF16 = jnp.bfloat16

N_MIXERS = 2
ATTN_HEADS = 8
ATTN_HEAD_DIM = 64
HEAD_WIDTH = 2 * ATTN_HEAD_DIM
LRU_BLOCKS = 4
CONV_WIDTH = 4
LRU_C = 8.0
NORM_EPS = 1e-6
LOG2E = math.log2(math.e)
MASKED_SCORE = -1e30

V7X_VMEM_BYTES = 64 * 1024 * 1024
V7X_SUBLANES = 8
V7X_LANES = 128

TOKEN_TILE = 512
PROJ_N_CHUNK = 512
ATTN_Q_TILE = 256
LRU_TIME_TILE = 256


def _vmem_limit(block_bytes, temp_bytes):
    return int(min(V7X_VMEM_BYTES - (4 << 20), 2 * block_bytes + temp_bytes + (4 << 20)))


def _rms_scale(x):
    return lax.rsqrt(jnp.mean(x * x, axis=-1, keepdims=True) + NORM_EPS)


def _norm_proj_kernel(x_ref, nw_ref, w_ref, o_ref, *, scaled_cols, col_scale):
    x = x_ref[...]
    xn = ((x * _rms_scale(x)) * nw_ref[...]).astype(BF16)
    n_out = o_ref.shape[-1]
    for n0 in range(0, n_out, PROJ_N_CHUNK):
        acc = jnp.dot(xn, w_ref[:, n0:n0 + PROJ_N_CHUNK], preferred_element_type=F32)
        if n0 < scaled_cols:
            acc = acc * col_scale
        o_ref[:, n0:n0 + PROJ_N_CHUNK] = acc.astype(o_ref.dtype)


def _norm_proj(x2d, norm_w, w_bf16, *, scaled_cols=0, col_scale=1.0):
    tokens, d = x2d.shape
    n_out = w_bf16.shape[1]
    assert tokens % TOKEN_TILE == 0 and n_out % PROJ_N_CHUNK == 0
    assert scaled_cols % PROJ_N_CHUNK == 0
    blocks = TOKEN_TILE * d * 4 + d * 4 + d * n_out * 2 + TOKEN_TILE * n_out * 2
    temps = TOKEN_TILE * d * 6 + 2 * TOKEN_TILE * PROJ_N_CHUNK * 4
    return pl.pallas_call(
        functools.partial(_norm_proj_kernel, scaled_cols=scaled_cols, col_scale=col_scale),
        grid=(tokens // TOKEN_TILE,),
        in_specs=[
            pl.BlockSpec((TOKEN_TILE, d), lambda i: (i, 0)),
            pl.BlockSpec((1, d), lambda i: (0, 0)),
            pl.BlockSpec((d, n_out), lambda i: (0, 0)),
        ],
        out_specs=pl.BlockSpec((TOKEN_TILE, n_out), lambda i: (i, 0)),
        out_shape=jax.ShapeDtypeStruct((tokens, n_out), BF16),
        compiler_params=pltpu.CompilerParams(
            dimension_semantics=("arbitrary",),
            vmem_limit_bytes=_vmem_limit(blocks, temps)),
        name="norm_proj",
    )(x2d, norm_w.reshape(1, d), w_bf16)


def _out_proj_kernel(y_ref, w_ref, pw_ref, x_ref, o_ref):
    out = jnp.dot(y_ref[...], w_ref[...], preferred_element_type=F32)
    o_ref[...] = x_ref[...] + (out * _rms_scale(out)) * pw_ref[...]


def _out_proj(y2d, w_bf16, post_w, x2d):
    tokens, d = x2d.shape
    width = y2d.shape[1]
    blocks = TOKEN_TILE * width * 2 + width * d * 2 + d * 4 + 2 * TOKEN_TILE * d * 4
    temps = 2 * TOKEN_TILE * d * 4
    return pl.pallas_call(
        _out_proj_kernel,
        grid=(tokens // TOKEN_TILE,),
        in_specs=[
            pl.BlockSpec((TOKEN_TILE, width), lambda i: (i, 0)),
            pl.BlockSpec((width, d), lambda i: (0, 0)),
            pl.BlockSpec((1, d), lambda i: (0, 0)),
            pl.BlockSpec((TOKEN_TILE, d), lambda i: (i, 0)),
        ],
        out_specs=pl.BlockSpec((TOKEN_TILE, d), lambda i: (i, 0)),
        out_shape=jax.ShapeDtypeStruct((tokens, d), F32),
        compiler_params=pltpu.CompilerParams(
            dimension_semantics=("arbitrary",),
            vmem_limit_bytes=_vmem_limit(blocks, temps)),
        name="out_proj",
    )(y2d, w_bf16, post_w.reshape(1, d), x2d)


def _diff_attn_kernel(lq1_ref, lk1_ref, lq2_ref, lk2_ref, sw_ref, q_ref, k_ref, v_ref, g_ref,
                      o_ref, *, lam_init, n_heads):
    seq = q_ref.shape[0]
    tq = ATTN_Q_TILE
    n_tiles = seq // tq
    head = pl.program_id(1)

    lam = (jnp.exp(jnp.sum(lq1_ref[...] * lk1_ref[...], keepdims=True))
           - jnp.exp(jnp.sum(lq2_ref[...] * lk2_ref[...], keepdims=True)) + lam_init)

    head_p1 = (head + 1).astype(F32)
    slope_col = jnp.exp2(jnp.full((2 * tq, 1), -(8.0 / n_heads), F32) * head_p1) * LOG2E
    slope_row = jnp.exp2(jnp.full((1, tq), -(8.0 / n_heads), F32) * head_p1) * LOG2E
    bias_row = slope_row * lax.broadcasted_iota(jnp.int32, (1, tq), 1).astype(F32)

    lane = lax.broadcasted_iota(jnp.int32, (tq, HEAD_WIDTH), 1)

    def q_tile(qi, _):
        q0 = pl.multiple_of(qi * tq, tq)
        q = q_ref[pl.ds(q0, tq), :]
        zero = jnp.zeros_like(q)
        q_stacked = jnp.concatenate(
            [jnp.where(lane < ATTN_HEAD_DIM, q, zero), jnp.where(lane >= ATTN_HEAD_DIM, q, zero)],
            axis=0)

        def kv_tile(kj, carry, masked):
            m, l, acc = carry
            k0 = pl.multiple_of(kj * tq, tq)
            k = k_ref[pl.ds(k0, tq), :]
            v = v_ref[pl.ds(k0, tq), :]
            s = lax.dot_general(q_stacked, k, (((1,), (1,)), ((), ())),
                                preferred_element_type=F32) + bias_row
            if masked:
                row = lax.broadcasted_iota(jnp.int32, (2 * tq, tq), 0) & (tq - 1)
                col = lax.broadcasted_iota(jnp.int32, (2 * tq, tq), 1)
                s = jnp.where(col <= row, s, MASKED_SCORE)
            off = slope_col * (k0 - q0).astype(F32)
            m_loc = m - off
            m_new = jnp.maximum(m_loc, jnp.max(s, axis=-1, keepdims=True))
            e = jnp.exp2(s - m_new)
            alpha = jnp.exp2(m_loc - m_new)
            l = alpha * l + jnp.sum(e, axis=-1, keepdims=True)
            acc = alpha * acc + jnp.dot(e.astype(BF16), v, preferred_element_type=F32)
            return m_new + off, l, acc

        init = (jnp.full((2 * tq, 1), MASKED_SCORE, F32), jnp.zeros((2 * tq, 1), F32),
                jnp.zeros((2 * tq, HEAD_WIDTH), F32))
        carry = lax.fori_loop(0, qi, functools.partial(kv_tile, masked=False), init)
        _, l, acc = kv_tile(qi, carry, masked=True)

        o = acc[:tq] / l[:tq] - lam * (acc[tq:] / l[tq:])
        o = ((o * _rms_scale(o)) * sw_ref[...]) * (1.0 - lam_init)
        g = g_ref[pl.ds(q0, tq), :].astype(F32)
        o_ref[pl.ds(q0, tq), :] = (o * (g * jax.nn.sigmoid(g))).astype(o_ref.dtype)
        return 0

    lax.fori_loop(0, n_tiles, q_tile, 0)


def _diff_attention(proj, lq1, lk1, lq2, lk2, subln_w, lam_init):
    batch, seq, n_proj = proj.shape
    width = n_proj // 4
    n_heads = width // HEAD_WIDTH
    assert seq % ATTN_Q_TILE == 0
    d = lq1.shape[-1]

    def head_block(section):
        return pl.BlockSpec((None, seq, HEAD_WIDTH),
                            lambda b, h, s=section: (b, 0, s * n_heads + h))

    def small(n):
        return pl.BlockSpec((1, n), lambda b, h: (0, 0))

    blocks = 5 * seq * HEAD_WIDTH * 2
    temps = 16 * (2 * ATTN_Q_TILE) * ATTN_Q_TILE * 4
    return pl.pallas_call(
        functools.partial(_diff_attn_kernel, lam_init=lam_init, n_heads=n_heads),
        grid=(batch, n_heads),
        in_specs=[small(d), small(d), small(d), small(d), small(HEAD_WIDTH),
                  head_block(0), head_block(1), head_block(2), head_block(3)],
        out_specs=pl.BlockSpec((None, seq, HEAD_WIDTH), lambda b, h: (b, 0, h)),
        out_shape=jax.ShapeDtypeStruct((batch, seq, width), BF16),
        compiler_params=pltpu.CompilerParams(
            dimension_semantics=("arbitrary", "arbitrary"),
            vmem_limit_bytes=_vmem_limit(blocks, temps)),
        name="diff_attn",
    )(lq1.reshape(1, d), lk1.reshape(1, d), lq2.reshape(1, d), lk2.reshape(1, d),
      subln_w.reshape(1, HEAD_WIDTH), proj, proj, proj, proj)


def _neg_expm1(x):
    u = jnp.exp(x)
    near = jnp.logical_and(x > -1.0, u < 1.0)
    near_value = (1.0 - u) * x / jnp.log(jnp.where(near, u, 0.5))
    return jnp.where(x <= -1.0, 1.0 - u, jnp.where(near, near_value, -x))


def _rglru_kernel(xr_ref, g_ref, cw_ref, cb_ref, wa_ref, ba_ref, wx_ref, bx_ref, la_ref,
                  o_ref, xs_ref, hc_ref, h_ref):
    tt, c = xr_ref.shape
    sub = V7X_SUBLANES
    blk = c // LRU_BLOCKS

    @pl.when(pl.program_id(1) == 0)
    def _():
        xs_ref[0:sub, :] = jnp.zeros((sub, c), F32)
        hc_ref[...] = jnp.zeros_like(hc_ref)

    xs_ref[sub:sub + tt, :] = xr_ref[...].astype(F32)
    xc = cb_ref[...]
    for j in range(CONV_WIDTH):
        xc = xc + cw_ref[j:j + 1, :] * xs_ref[pl.ds(sub - (CONV_WIDTH - 1) + j, tt), :]
    xs_ref[0:sub, :] = xs_ref[tt:tt + sub, :]

    xc_bf = xc.astype(BF16)

    def gate(w_ref, b_ref):
        parts = [jnp.dot(xc_bf[:, g * blk:(g + 1) * blk], w_ref[g], preferred_element_type=F32)
                 for g in range(LRU_BLOCKS)]
        return jax.nn.sigmoid(jnp.concatenate(parts, axis=-1) + b_ref[...])

    r = gate(wa_ref, ba_ref)
    i = gate(wx_ref, bx_ref)
    z = -la_ref[...]
    softplus = jnp.maximum(z, 0.0) + jnp.log1p(jnp.exp(-jnp.abs(z)))
    log_a = (-LRU_C * r) * softplus
    a = jnp.exp(log_a)
    b = jnp.sqrt(_neg_expm1(2.0 * log_a)) * (i * xc)

    row_in_group = lax.broadcasted_iota(jnp.int32, (tt, c), 0) & (sub - 1)
    shift = 1
    while shift < sub:
        valid = row_in_group >= shift
        a_prev = jnp.where(valid, pltpu.roll(a, shift, 0), 1.0)
        b_prev = jnp.where(valid, pltpu.roll(b, shift, 0), 0.0)
        b = a * b_prev + b
        a = a * a_prev
        shift *= 2
    h_last = hc_ref[...]
    for g0 in range(0, tt, sub):
        h = a[g0:g0 + sub] * h_last + b[g0:g0 + sub]
        h_ref[g0:g0 + sub, :] = h
        h_last = h[sub - 1:sub]
    hc_ref[...] = h_last

    g = g_ref[...].astype(F32)
    o_ref[...] = (h_ref[...] * (g * jax.nn.sigmoid(g))).astype(o_ref.dtype)


def _rglru(proj, conv_w, conv_b, wa_bf16, ba, wx_bf16, bx, log_a_param):
    batch, seq, n_proj = proj.shape
    c = n_proj // 2
    tt = LRU_TIME_TILE
    blk = c // LRU_BLOCKS
    assert seq % tt == 0 and tt % V7X_SUBLANES == 0

    def row(n):
        return pl.BlockSpec((1, n), lambda b, t: (0, 0))

    w_spec = pl.BlockSpec((LRU_BLOCKS, blk, blk), lambda b, t: (0, 0, 0))
    blocks = 3 * tt * c * 2 + 2 * LRU_BLOCKS * blk * blk * 2 + 8 * c * 4
    temps = 14 * tt * c * 4
    return pl.pallas_call(
        _rglru_kernel,
        grid=(batch, seq // tt),
        in_specs=[
            pl.BlockSpec((None, tt, c), lambda b, t: (b, t, 0)),
            pl.BlockSpec((None, tt, c), lambda b, t: (b, t, 1)),
            pl.BlockSpec((CONV_WIDTH, c), lambda b, t: (0, 0)),
            row(c), w_spec, row(c), w_spec, row(c), row(c),
        ],
        out_specs=pl.BlockSpec((None, tt, c), lambda b, t: (b, t, 0)),
        out_shape=jax.ShapeDtypeStruct((batch, seq, c), BF16),
        scratch_shapes=[
            pltpu.VMEM((tt + 2 * V7X_SUBLANES, c), F32),
            pltpu.VMEM((1, c), F32),
            pltpu.VMEM((tt, c), F32),
        ],
        compiler_params=pltpu.CompilerParams(
            dimension_semantics=("arbitrary", "arbitrary"),
            vmem_limit_bytes=_vmem_limit(blocks, temps)),
        name="rglru",
    )(proj, proj, conv_w, conv_b.reshape(1, c), wa_bf16, ba.reshape(1, c), wx_bf16,
      bx.reshape(1, c), log_a_param.reshape(1, c))


def kernel(x, pre_norm_w, post_norm_w, attn_w_in, attn_w_out, attn_lambda_q1, attn_lambda_k1,
           attn_lambda_q2, attn_lambda_k2, attn_subln_w, lru_w_in, lru_conv_w, lru_conv_b,
           lru_gate_a_w, lru_gate_a_b, lru_gate_x_w, lru_gate_x_b, lru_log_a_param, lru_w_out):
    batch, seq, d = x.shape
    depth = pre_norm_w.shape[0]
    x2d = x.reshape(batch * seq, d)
    for layer in range(depth):
        j = layer // N_MIXERS
        if layer % N_MIXERS == 0:
            width = attn_w_out.shape[1]
            lam_init = 0.8 - 0.6 * math.exp(-0.3 * layer)
            proj = _norm_proj(x2d, pre_norm_w[layer], attn_w_in[j].astype(BF16),
                              scaled_cols=width, col_scale=ATTN_HEAD_DIM ** -0.5 * LOG2E)
            y = _diff_attention(proj.reshape(batch, seq, 4 * width), attn_lambda_q1[j],
                                attn_lambda_k1[j], attn_lambda_q2[j], attn_lambda_k2[j],
                                attn_subln_w[j], lam_init)
            w_out = attn_w_out[j]
        else:
            width = lru_w_out.shape[1]
            proj = _norm_proj(x2d, pre_norm_w[layer], lru_w_in[j].astype(BF16))
            y = _rglru(proj.reshape(batch, seq, 2 * width), lru_conv_w[j], lru_conv_b[j],
                       lru_gate_a_w[j].astype(BF16), lru_gate_a_b[j],
                       lru_gate_x_w[j].astype(BF16), lru_gate_x_b[j], lru_log_a_param[j])
            w_out = lru_w_out[j]
        x2d = _out_proj(y.reshape(batch * seq, width), w_out.astype(BF16), post_norm_w[layer], x2d)
    return x2d.reshape(batch, seq, d)
```

```python
import functools
import math

import jax
import jax.numpy as jnp
from jax import lax
from jax.experimental import pallas as pl
from jax.experimental.pallas import tpu as pltpu

F32 = jnp.float32
BF16 = jnp.bfloat16

N_MIXERS = 2
ATTN_HEADS = 8
ATTN_HEAD_DIM = 64
HEAD_WIDTH = 2 * ATTN_HEAD_DIM
LRU_BLOCKS = 4
CONV_WIDTH = 4
LRU_C = 8.0
NORM_EPS = 1e-6
LOG2E = math.log2(math.e)
MASKED_SCORE = -1e30

V7X_VMEM_BYTES = 64 * 1024 * 1024
V7X_SUBLANES = 8
V7X_LANES = 128

TOKEN_TILE = 512
PROJ_N_CHUNK = 512
ATTN_Q_TILE = 256
LRU_TIME_TILE = 256


def _vmem_limit(block_bytes, temp_bytes):
    return int(min(V7X_VMEM_BYTES - (4 << 20), 2 * block_bytes + temp_bytes + (4 << 20)))


def _rms_scale(x):
    return lax.rsqrt(jnp.mean(x * x, axis=-1, keepdims=True) + NORM_EPS)


def _norm_proj_kernel(x_ref, nw_ref, w_ref, o_ref, *, scaled_cols, col_scale):
    x = x_ref[...]
    xn = ((x * _rms_scale(x)) * nw_ref[...]).astype(BF16)
    n_out = o_ref.shape[-1]
    for n0 in range(0, n_out, PROJ_N_CHUNK):
        acc = jnp.dot(xn, w_ref[:, n0:n0 + PROJ_N_CHUNK], preferred_element_type=F32)
        if n0 < scaled_cols:
            acc = acc * col_scale
        o_ref[:, n0:n0 + PROJ_N_CHUNK] = acc.astype(o_ref.dtype)


def _norm_proj(x2d, norm_w, w_bf16, *, scaled_cols=0, col_scale=1.0):
    tokens, d = x2d.shape
    n_out = w_bf16.shape[1]
    assert tokens % TOKEN_TILE == 0 and n_out % PROJ_N_CHUNK == 0
    assert scaled_cols % PROJ_N_CHUNK == 0
    blocks = TOKEN_TILE * d * 4 + d * 4 + d * n_out * 2 + TOKEN_TILE * n_out * 2
    temps = TOKEN_TILE * d * 6 + 2 * TOKEN_TILE * PROJ_N_CHUNK * 4
    return pl.pallas_call(
        functools.partial(_norm_proj_kernel, scaled_cols=scaled_cols, col_scale=col_scale),
        grid=(tokens // TOKEN_TILE,),
        in_specs=[
            pl.BlockSpec((TOKEN_TILE, d), lambda i: (i, 0)),
            pl.BlockSpec((1, d), lambda i: (0, 0)),
            pl.BlockSpec((d, n_out), lambda i: (0, 0)),
        ],
        out_specs=pl.BlockSpec((TOKEN_TILE, n_out), lambda i: (i, 0)),
        out_shape=jax.ShapeDtypeStruct((tokens, n_out), BF16),
        compiler_params=pltpu.CompilerParams(
            dimension_semantics=("arbitrary",),
            vmem_limit_bytes=_vmem_limit(blocks, temps)),
        name="norm_proj",
    )(x2d, norm_w.reshape(1, d), w_bf16)


def _out_proj_kernel(y_ref, w_ref, pw_ref, x_ref, o_ref):
    out = jnp.dot(y_ref[...], w_ref[...], preferred_element_type=F32)
    o_ref[...] = x_ref[...] + (out * _rms_scale(out)) * pw_ref[...]


def _out_proj(y2d, w_bf16, post_w, x2d):
    tokens, d = x2d.shape
    width = y2d.shape[1]
    blocks = TOKEN_TILE * width * 2 + width * d * 2 + d * 4 + 2 * TOKEN_TILE * d * 4
    temps = 2 * TOKEN_TILE * d * 4
    return pl.pallas_call(
        _out_proj_kernel,
        grid=(tokens // TOKEN_TILE,),
        in_specs=[
            pl.BlockSpec((TOKEN_TILE, width), lambda i: (i, 0)),
            pl.BlockSpec((width, d), lambda i: (0, 0)),
            pl.BlockSpec((1, d), lambda i: (0, 0)),
            pl.BlockSpec((TOKEN_TILE, d), lambda i: (i, 0)),
        ],
        out_specs=pl.BlockSpec((TOKEN_TILE, d), lambda i: (i, 0)),
        out_shape=jax.ShapeDtypeStruct((tokens, d), F32),
        compiler_params=pltpu.CompilerParams(
            dimension_semantics=("arbitrary",),
            vmem_limit_bytes=_vmem_limit(blocks, temps)),
        name="out_proj",
    )(y2d, w_bf16, post_w.reshape(1, d), x2d)


def _diff_attn_kernel(lq1_ref, lk1_ref, lq2_ref, lk2_ref, sw_ref, q_ref, k_ref, v_ref, g_ref,
                      o_ref, ka_ref, vt_ref, *, lam_init, n_heads):
    seq = q_ref.shape[0]
    tq = ATTN_Q_TILE
    n_tiles = seq // tq
    head = pl.program_id(1)

    lam = (jnp.exp(jnp.sum(lq1_ref[...] * lk1_ref[...], keepdims=True))
           - jnp.exp(jnp.sum(lq2_ref[...] * lk2_ref[...], keepdims=True)) + lam_init)

    head_p1 = (head + 1).astype(F32)
    slope = jnp.exp2(jnp.full((seq, HEAD_WIDTH), -(8.0 / n_heads), F32) * head_p1) * LOG2E
    key_pos = lax.broadcasted_iota(jnp.int32, (seq, HEAD_WIDTH), 0).astype(F32)
    key_lane = lax.broadcasted_iota(jnp.int32, (seq, HEAD_WIDTH), 1)
    bias = slope * key_pos
    bias_hi = bias.astype(BF16).astype(F32)
    rest = bias - bias_hi
    bias_mid = rest.astype(BF16).astype(F32)
    bias_lo = rest - bias_mid
    pieces = jnp.where(key_lane == 0, bias_hi,
                       jnp.where(key_lane == 1, bias_mid,
                                 jnp.where(key_lane == 2, bias_lo, 0.0)))
    ka_ref[:, 0:HEAD_WIDTH] = k_ref[...]
    ka_ref[:, HEAD_WIDTH:2 * HEAD_WIDTH] = pieces.astype(BF16)
    vt_ref[...] = v_ref[...].T

    lane = lax.broadcasted_iota(jnp.int32, (tq, HEAD_WIDTH), 1)
    ones = jnp.where(lane < 3, 1.0, 0.0).astype(BF16)

    for qi in range(n_tiles):
        q = q_ref[qi * tq:(qi + 1) * tq, :]
        zero = jnp.zeros_like(q)
        q_stacked = jnp.concatenate(
            [jnp.concatenate([jnp.where(lane < ATTN_HEAD_DIM, q, zero), ones], axis=1),
             jnp.concatenate([jnp.where(lane >= ATTN_HEAD_DIM, q, zero), ones], axis=1)],
            axis=0)
        m = jnp.full((1, 2 * tq), MASKED_SCORE, F32)
        l = jnp.zeros((1, 2 * tq), F32)
        acc = jnp.zeros((HEAD_WIDTH, 2 * tq), F32)
        for kj in range(qi + 1):
            keys = ka_ref[kj * tq:(kj + 1) * tq, :]
            s = lax.dot_general(keys, q_stacked, (((1,), (1,)), ((), ())),
                                preferred_element_type=F32)
            if kj == qi:
                key_row = lax.broadcasted_iota(jnp.int32, (tq, 2 * tq), 0)
                query_col = lax.broadcasted_iota(jnp.int32, (tq, 2 * tq), 1) & (tq - 1)
                s = jnp.where(key_row <= query_col, s, MASKED_SCORE)
            m_new = jnp.maximum(m, jnp.max(s, axis=0, keepdims=True))
            e = jnp.exp2(s - m_new)
            alpha = jnp.exp2(m - m_new)
            l = alpha * l + jnp.sum(e, axis=0, keepdims=True)
            acc = alpha * acc + jnp.dot(vt_ref[:, kj * tq:(kj + 1) * tq], e.astype(BF16),
                                        preferred_element_type=F32)
            m = m_new

        o_t = acc[:, :tq] / l[:, :tq] - lam * (acc[:, tq:] / l[:, tq:])
        o_t = o_t * lax.rsqrt(jnp.mean(o_t * o_t, axis=0, keepdims=True) + NORM_EPS)
        o = (o_t.T * sw_ref[...]) * (1.0 - lam_init)
        g = g_ref[qi * tq:(qi + 1) * tq, :].astype(F32)
        o_ref[qi * tq:(qi + 1) * tq, :] = (o * (g * jax.nn.sigmoid(g))).astype(o_ref.dtype)


def _diff_attention(proj, lq1, lk1, lq2, lk2, subln_w, lam_init):
    batch, seq, n_proj = proj.shape
    width = n_proj // 4
    n_heads = width // HEAD_WIDTH
    assert seq % ATTN_Q_TILE == 0
    d = lq1.shape[-1]

    def head_block(section):
        return pl.BlockSpec((None, seq, HEAD_WIDTH),
                            lambda b, h, s=section: (b, 0, s * n_heads + h))

    def small(n):
        return pl.BlockSpec((1, n), lambda b, h: (0, 0))

    blocks = 5 * seq * HEAD_WIDTH * 2
    temps = 3 * seq * HEAD_WIDTH * 2 + 32 * (2 * ATTN_Q_TILE) * ATTN_Q_TILE * 4
    return pl.pallas_call(
        functools.partial(_diff_attn_kernel, lam_init=lam_init, n_heads=n_heads),
        grid=(batch, n_heads),
        in_specs=[small(d), small(d), small(d), small(d), small(HEAD_WIDTH),
                  head_block(0), head_block(1), head_block(2), head_block(3)],
        out_specs=pl.BlockSpec((None, seq, HEAD_WIDTH), lambda b, h: (b, 0, h)),
        out_shape=jax.ShapeDtypeStruct((batch, seq, width), BF16),
        scratch_shapes=[
            pltpu.VMEM((seq, 2 * HEAD_WIDTH), BF16),
            pltpu.VMEM((HEAD_WIDTH, seq), BF16),
        ],
        compiler_params=pltpu.CompilerParams(
            dimension_semantics=("arbitrary", "arbitrary"),
            vmem_limit_bytes=_vmem_limit(blocks, temps)),
        name="diff_attn",
    )(lq1.reshape(1, d), lk1.reshape(1, d), lq2.reshape(1, d), lk2.reshape(1, d),
      subln_w.reshape(1, HEAD_WIDTH), proj, proj, proj, proj)


def _neg_expm1(x):
    u = jnp.exp(x)
    near = jnp.logical_and(x > -1.0, u < 1.0)
    near_value = (1.0 - u) * x / jnp.log(jnp.where(near, u, 0.5))
    return jnp.where(x <= -1.0, 1.0 - u, jnp.where(near, near_value, -x))


def _rglru_kernel(xr_ref, g_ref, cw_ref, cb_ref, wa_ref, ba_ref, wx_ref, bx_ref, la_ref,
                  o_ref, xs_ref, hc_ref, h_ref):
    tt, c = xr_ref.shape
    sub = V7X_SUBLANES
    blk = c // LRU_BLOCKS

    @pl.when(pl.program_id(1) == 0)
    def _():
        xs_ref[0:sub, :] = jnp.zeros((sub, c), F32)
        hc_ref[...] = jnp.zeros_like(hc_ref)

    xs_ref[sub:sub + tt, :] = xr_ref[...].astype(F32)
    xc = cb_ref[...]
    for j in range(CONV_WIDTH):
        xc = xc + cw_ref[j:j + 1, :] * xs_ref[pl.ds(sub - (CONV_WIDTH - 1) + j, tt), :]
    xs_ref[0:sub, :] = xs_ref[tt:tt + sub, :]

    xc_bf = xc.astype(BF16)

    def gate(w_ref, b_ref):
        parts = [jnp.dot(xc_bf[:, g * blk:(g + 1) * blk], w_ref[g], preferred_element_type=F32)
                 for g in range(LRU_BLOCKS)]
        return jax.nn.sigmoid(jnp.concatenate(parts, axis=-1) + b_ref[...])

    r = gate(wa_ref, ba_ref)
    i = gate(wx_ref, bx_ref)
    z = -la_ref[...]
    softplus = jnp.maximum(z, 0.0) + jnp.log1p(jnp.exp(-jnp.abs(z)))
    log_a = (-LRU_C * r) * softplus
    a = jnp.exp(log_a)
    b = jnp.sqrt(_neg_expm1(2.0 * log_a)) * (i * xc)

    row_in_group = lax.broadcasted_iota(jnp.int32, (tt, c), 0) & (sub - 1)
    shift = 1
    while shift < sub:
        valid = row_in_group >= shift
        a_prev = jnp.where(valid, pltpu.roll(a, shift, 0), 1.0)
        b_prev = jnp.where(valid, pltpu.roll(b, shift, 0), 0.0)
        b = a * b_prev + b
        a = a * a_prev
        shift *= 2
    h_last = hc_ref[...]
    for g0 in range(0, tt, sub):
        h = a[g0:g0 + sub] * h_last + b[g0:g0 + sub]
        h_ref[g0:g0 + sub, :] = h
        h_last = h[sub - 1:sub]
    hc_ref[...] = h_last

    g = g_ref[...].astype(F32)
    o_ref[...] = (h_ref[...] * (g * jax.nn.sigmoid(g))).astype(o_ref.dtype)


def _rglru(proj, conv_w, conv_b, wa_bf16, ba, wx_bf16, bx, log_a_param):
    batch, seq, n_proj = proj.shape
    c = n_proj // 2
    tt = LRU_TIME_TILE
    blk = c // LRU_BLOCKS
    assert seq % tt == 0 and tt % V7X_SUBLANES == 0

    def row(n):
        return pl.BlockSpec((1, n), lambda b, t: (0, 0))

    w_spec = pl.BlockSpec((LRU_BLOCKS, blk, blk), lambda b, t: (0, 0, 0))
    blocks = 3 * tt * c * 2 + 2 * LRU_BLOCKS * blk * blk * 2 + 8 * c * 4
    temps = 14 * tt * c * 4
    return pl.pallas_call(
        _rglru_kernel,
        grid=(batch, seq // tt),
        in_specs=[
            pl.BlockSpec((None, tt, c), lambda b, t: (b, t, 0)),
            pl.BlockSpec((None, tt, c), lambda b, t: (b, t, 1)),
            pl.BlockSpec((CONV_WIDTH, c), lambda b, t: (0, 0)),
            row(c), w_spec, row(c), w_spec, row(c), row(c),
        ],
        out_specs=pl.BlockSpec((None, tt, c), lambda b, t: (b, t, 0)),
        out_shape=jax.ShapeDtypeStruct((batch, seq, c), BF16),
        scratch_shapes=[
            pltpu.VMEM((tt + 2 * V7X_SUBLANES, c), F32),
            pltpu.VMEM((1, c), F32),
            pltpu.VMEM((tt, c), F32),
        ],
        compiler_params=pltpu.CompilerParams(
            dimension_semantics=("arbitrary", "arbitrary"),
            vmem_limit_bytes=_vmem_limit(blocks, temps)),
        name="rglru",
    )(proj, proj, conv_w, conv_b.reshape(1, c), wa_bf16, ba.reshape(1, c), wx_bf16,
      bx.reshape(1, c), log_a_param.reshape(1, c))


def kernel(x, pre_norm_w, post_norm_w, attn_w_in, attn_w_out, attn_lambda_q1, attn_lambda_k1,
           attn_lambda_q2, attn_lambda_k2, attn_subln_w, lru_w_in, lru_conv_w, lru_conv_b,
           lru_gate_a_w, lru_gate_a_b, lru_gate_x_w, lru_gate_x_b, lru_log_a_param, lru_w_out):
    batch, seq, d = x.shape
    depth = pre_norm_w.shape[0]
    x2d = x.reshape(batch * seq, d)
    for layer in range(depth):
        j = layer // N_MIXERS
        if layer % N_MIXERS == 0:
            width = attn_w_out.shape[1]
            lam_init = 0.8 - 0.6 * math.exp(-0.3 * layer)
            proj = _norm_proj(x2d, pre_norm_w[layer], attn_w_in[j].astype(BF16),
                              scaled_cols=width, col_scale=ATTN_HEAD_DIM ** -0.5 * LOG2E)
            y = _diff_attention(proj.reshape(batch, seq, 4 * width), attn_lambda_q1[j],
                                attn_lambda_k1[j], attn_lambda_q2[j], attn_lambda_k2[j],
                                attn_subln_w[j], lam_init)
            w_out = attn_w_out[j]
        else:
            width = lru_w_out.shape[1]
            proj = _norm_proj(x2d, pre_norm_w[layer], lru_w_in[j].astype(BF16))
            y = _rglru(proj.reshape(batch, seq, 2 * width), lru_conv_w[j], lru_conv_b[j],
                       lru_gate_a_w[j].astype(BF16), lru_gate_a_b[j],
                       lru_gate_x_w[j].astype(BF16), lru_gate_x_b[j], lru_log_a_param[j])
            w_out = lru_w_out[j]
        x2d = _out_proj(y.reshape(batch * seq, width), w_out.astype(BF16), post_norm_w[layer], x2d)
    return x2d.reshape(batch, seq, d)
```

```python
import functools
import math

import jax
import jax.numpy as jnp
from jax import lax
from jax.experimental import pallas as pl
from jax.experimental.pallas import tpu as pltpu

F32 = jnp.float32
BF16 = jnp.bfloat16

N_MIXERS = 2
ATTN_HEADS = 8
ATTN_HEAD_DIM = 64
HEAD_WIDTH = 2 * ATTN_HEAD_DIM
LRU_BLOCKS = 4
CONV_WIDTH = 4
LRU_C = 8.0
NORM_EPS = 1e-6
LOG2E = math.log2(math.e)
MASKED_SCORE = -1e30

V7X_VMEM_BYTES = 64 * 1024 * 1024
V7X_SUBLANES = 8
V7X_LANES = 128

TOKEN_TILE = 512
PROJ_N_CHUNK = 512
ATTN_Q_TILE = 256
LRU_TIME_TILE = 256


def _vmem_limit(block_bytes, temp_bytes):
    return int(min(V7X_VMEM_BYTES - (4 << 20), 2 * block_bytes + temp_bytes + (4 << 20)))


def _rms_scale(x):
    return lax.rsqrt(jnp.mean(x * x, axis=-1, keepdims=True) + NORM_EPS)


def _norm_proj_kernel(x_ref, nw_ref, w_ref, o_ref, *, scaled_cols, col_scale):
    x = x_ref[...]
    xn = ((x * _rms_scale(x)) * nw_ref[...]).astype(BF16)
    n_out = o_ref.shape[-1]
    for n0 in range(0, n_out, PROJ_N_CHUNK):
        acc = jnp.dot(xn, w_ref[:, n0:n0 + PROJ_N_CHUNK], preferred_element_type=F32)
        if n0 < scaled_cols:
            acc = acc * col_scale
        o_ref[:, n0:n0 + PROJ_N_CHUNK] = acc.astype(o_ref.dtype)


def _norm_proj(x2d, norm_w, w_bf16, *, scaled_cols=0, col_scale=1.0):
    tokens, d = x2d.shape
    n_out = w_bf16.shape[1]
    assert tokens % TOKEN_TILE == 0 and n_out % PROJ_N_CHUNK == 0
    assert scaled_cols % PROJ_N_CHUNK == 0
    blocks = TOKEN_TILE * d * 4 + d * 4 + d * n_out * 2 + TOKEN_TILE * n_out * 2
    temps = TOKEN_TILE * d * 6 + 2 * TOKEN_TILE * PROJ_N_CHUNK * 4
    return pl.pallas_call(
        functools.partial(_norm_proj_kernel, scaled_cols=scaled_cols, col_scale=col_scale),
        grid=(tokens // TOKEN_TILE,),
        in_specs=[
            pl.BlockSpec((TOKEN_TILE, d), lambda i: (i, 0)),
            pl.BlockSpec((1, d), lambda i: (0, 0)),
            pl.BlockSpec((d, n_out), lambda i: (0, 0)),
        ],
        out_specs=pl.BlockSpec((TOKEN_TILE, n_out), lambda i: (i, 0)),
        out_shape=jax.ShapeDtypeStruct((tokens, n_out), BF16),
        compiler_params=pltpu.CompilerParams(
            dimension_semantics=("arbitrary",),
            vmem_limit_bytes=_vmem_limit(blocks, temps)),
        name="norm_proj",
    )(x2d, norm_w.reshape(1, d), w_bf16)


def _out_proj_kernel(y_ref, w_ref, pw_ref, x_ref, o_ref):
    out = jnp.dot(y_ref[...], w_ref[...], preferred_element_type=F32)
    o_ref[...] = x_ref[...] + (out * _rms_scale(out)) * pw_ref[...]


def _out_proj(y2d, w_bf16, post_w, x2d):
    tokens, d = x2d.shape
    width = y2d.shape[1]
    blocks = TOKEN_TILE * width * 2 + width * d * 2 + d * 4 + 2 * TOKEN_TILE * d * 4
    temps = 2 * TOKEN_TILE * d * 4
    return pl.pallas_call(
        _out_proj_kernel,
        grid=(tokens // TOKEN_TILE,),
        in_specs=[
            pl.BlockSpec((TOKEN_TILE, width), lambda i: (i, 0)),
            pl.BlockSpec((width, d), lambda i: (0, 0)),
            pl.BlockSpec((1, d), lambda i: (0, 0)),
            pl.BlockSpec((TOKEN_TILE, d), lambda i: (i, 0)),
        ],
        out_specs=pl.BlockSpec((TOKEN_TILE, d), lambda i: (i, 0)),
        out_shape=jax.ShapeDtypeStruct((tokens, d), F32),
        compiler_params=pltpu.CompilerParams(
            dimension_semantics=("arbitrary",),
            vmem_limit_bytes=_vmem_limit(blocks, temps)),
        name="out_proj",
    )(y2d, w_bf16, post_w.reshape(1, d), x2d)


def _diff_attn_kernel(lq1_ref, lk1_ref, lq2_ref, lk2_ref, sw_ref, q_ref, k_ref, v_ref, g_ref,
                      o_ref, ka_ref, vt_ref, *, lam_init, n_heads):
    seq = q_ref.shape[0]
    tq = ATTN_Q_TILE
    n_tiles = seq // tq
    head = pl.program_id(1)

    lam = (jnp.exp(jnp.sum(lq1_ref[...] * lk1_ref[...], keepdims=True))
           - jnp.exp(jnp.sum(lq2_ref[...] * lk2_ref[...], keepdims=True)) + lam_init)

    head_p1 = (head + 1).astype(F32)
    slope = jnp.exp2(jnp.full((seq, HEAD_WIDTH), -(8.0 / n_heads), F32) * head_p1) * LOG2E
    key_pos = lax.broadcasted_iota(jnp.int32, (seq, HEAD_WIDTH), 0).astype(F32)
    key_lane = lax.broadcasted_iota(jnp.int32, (seq, HEAD_WIDTH), 1)
    bias = slope * key_pos
    bias_hi = bias.astype(BF16).astype(F32)
    rest = bias - bias_hi
    bias_mid = rest.astype(BF16).astype(F32)
    bias_lo = rest - bias_mid
    pieces = jnp.where(key_lane == 0, bias_hi,
                       jnp.where(key_lane == 1, bias_mid,
                                 jnp.where(key_lane == 2, bias_lo, 0.0)))
    ka_ref[:, 0:HEAD_WIDTH] = k_ref[...]
    ka_ref[:, HEAD_WIDTH:2 * HEAD_WIDTH] = pieces.astype(BF16)
    vt_ref[0:HEAD_WIDTH, :] = v_ref[...].T
    vt_ref[HEAD_WIDTH:, :] = jnp.ones((vt_ref.shape[0] - HEAD_WIDTH, seq), BF16)

    lane = lax.broadcasted_iota(jnp.int32, (tq, HEAD_WIDTH), 1)
    ones = jnp.where(lane < 3, 1.0, 0.0).astype(BF16)

    def stacked_queries(qi):
        q = q_ref[qi * tq:(qi + 1) * tq, :]
        zero = jnp.zeros_like(q)
        return jnp.concatenate(
            [jnp.concatenate([jnp.where(lane < ATTN_HEAD_DIM, q, zero), ones], axis=1),
             jnp.concatenate([jnp.where(lane >= ATTN_HEAD_DIM, q, zero), ones], axis=1)],
            axis=0)

    def scores(q_stacked, kj):
        keys = ka_ref[kj * tq:(kj + 1) * tq, :]
        return lax.dot_general(keys, q_stacked, (((1,), (1,)), ((), ())),
                               preferred_element_type=F32)

    tiles = [(qi, kj) for qi in range(n_tiles) for kj in range(qi + 1)]
    q_stacked = stacked_queries(0)
    s_next = scores(q_stacked, 0)
    for t, (qi, kj) in enumerate(tiles):
        s = s_next
        if t + 1 < len(tiles):
            qi_next, kj_next = tiles[t + 1]
            if qi_next != qi:
                q_stacked = stacked_queries(qi_next)
            s_next = scores(q_stacked, kj_next)
        if kj == 0:
            m = jnp.full((1, 2 * tq), MASKED_SCORE, F32)
            acc = jnp.zeros((vt_ref.shape[0], 2 * tq), F32)
        if kj == qi:
            key_row = lax.broadcasted_iota(jnp.int32, (tq, 2 * tq), 0)
            query_col = lax.broadcasted_iota(jnp.int32, (tq, 2 * tq), 1) & (tq - 1)
            s = jnp.where(key_row <= query_col, s, MASKED_SCORE)
        m_new = jnp.maximum(m, jnp.max(s, axis=0, keepdims=True))
        e = jnp.exp2(s - m_new)
        alpha = jnp.exp2(m - m_new)
        acc = alpha * acc + jnp.dot(vt_ref[:, kj * tq:(kj + 1) * tq], e.astype(BF16),
                                    preferred_element_type=F32)
        m = m_new
        if kj < qi:
            continue

        num = acc[:HEAD_WIDTH]
        den = acc[HEAD_WIDTH:HEAD_WIDTH + 1]
        o_t = num[:, :tq] / den[:, :tq] - lam * (num[:, tq:] / den[:, tq:])
        o_t = o_t * lax.rsqrt(jnp.mean(o_t * o_t, axis=0, keepdims=True) + NORM_EPS)
        o = (o_t.T * sw_ref[...]) * (1.0 - lam_init)
        g = g_ref[qi * tq:(qi + 1) * tq, :].astype(F32)
        o_ref[qi * tq:(qi + 1) * tq, :] = (o * (g * jax.nn.sigmoid(g))).astype(o_ref.dtype)


def _diff_attention(proj, lq1, lk1, lq2, lk2, subln_w, lam_init):
    batch, seq, n_proj = proj.shape
    width = n_proj // 4
    n_heads = width // HEAD_WIDTH
    assert seq % ATTN_Q_TILE == 0
    d = lq1.shape[-1]

    def head_block(section):
        return pl.BlockSpec((None, seq, HEAD_WIDTH),
                            lambda b, h, s=section: (b, 0, s * n_heads + h))

    def small(n):
        return pl.BlockSpec((1, n), lambda b, h: (0, 0))

    blocks = 5 * seq * HEAD_WIDTH * 2
    temps = 3 * seq * HEAD_WIDTH * 2 + 32 * (2 * ATTN_Q_TILE) * ATTN_Q_TILE * 4
    return pl.pallas_call(
        functools.partial(_diff_attn_kernel, lam_init=lam_init, n_heads=n_heads),
        grid=(batch, n_heads),
        in_specs=[small(d), small(d), small(d), small(d), small(HEAD_WIDTH),
                  head_block(0), head_block(1), head_block(2), head_block(3)],
        out_specs=pl.BlockSpec((None, seq, HEAD_WIDTH), lambda b, h: (b, 0, h)),
        out_shape=jax.ShapeDtypeStruct((batch, seq, width), BF16),
        scratch_shapes=[
            pltpu.VMEM((seq, 2 * HEAD_WIDTH), BF16),
            pltpu.VMEM((HEAD_WIDTH + 2 * V7X_SUBLANES, seq), BF16),
        ],
        compiler_params=pltpu.CompilerParams(
            dimension_semantics=("arbitrary", "arbitrary"),
            vmem_limit_bytes=_vmem_limit(blocks, temps)),
        name="diff_attn",
    )(lq1.reshape(1, d), lk1.reshape(1, d), lq2.reshape(1, d), lk2.reshape(1, d),
      subln_w.reshape(1, HEAD_WIDTH), proj, proj, proj, proj)


def _sigmoid_of_scaled(x_scaled):
    return 1.0 / (1.0 + jnp.exp2(x_scaled))


def _one_minus_square(log2_a):
    t = jnp.tanh(log2_a * (1.0 / LOG2E))
    return (-2.0 * t) / (1.0 - t)


def _sqrt_nonneg(x):
    positive = x > 0.0
    return jnp.where(positive, x * lax.rsqrt(jnp.where(positive, x, 1.0)), 0.0)


def _rglru_kernel(xr_ref, g_ref, cw_ref, cb_ref, wa_ref, ba_ref, wx_ref, bx_ref, la_ref,
                  o_ref, halo_ref, hc_ref, xc_ref, za_ref, zx_ref, h_ref):
    tt, c = xr_ref.shape
    sub = V7X_SUBLANES
    blk = c // LRU_BLOCKS

    @pl.when(pl.program_id(1) == 0)
    def _():
        halo_ref[...] = jnp.zeros_like(halo_ref)
        hc_ref[...] = jnp.zeros_like(hc_ref)

    row = lax.broadcasted_iota(jnp.int32, (sub, c), 0)

    x_all = xr_ref[...].astype(F32)
    taps = [cw_ref[j:j + 1, :] for j in range(CONV_WIDTH)]
    conv_b = cb_ref[...]
    delays = range(1, CONV_WIDTH)
    prev_rolled = [pltpu.roll(halo_ref[...], d, 0) for d in delays]
    for g0 in range(0, tt, sub):
        x = x_all[g0:g0 + sub]
        rolled = [pltpu.roll(x, d, 0) for d in delays]
        xc = conv_b + taps[CONV_WIDTH - 1] * x
        for d in delays:
            delayed = jnp.where(row >= d, rolled[d - 1], prev_rolled[d - 1])
            xc = xc + taps[CONV_WIDTH - 1 - d] * delayed
        xc_ref[g0:g0 + sub, :] = xc
        prev_rolled = rolled
    halo_ref[...] = x_all[tt - sub:tt]

    xc_bf = xc_ref[...].astype(BF16)
    for blk_i in range(LRU_BLOCKS):
        cols = slice(blk_i * blk, (blk_i + 1) * blk)
        za_ref[:, cols] = jnp.dot(xc_bf[:, cols], wa_ref[blk_i], preferred_element_type=F32)
        zx_ref[:, cols] = jnp.dot(xc_bf[:, cols], wx_ref[blk_i], preferred_element_type=F32)

    z = -la_ref[...]
    softplus = jnp.maximum(z, 0.0) + jnp.log1p(jnp.exp(-jnp.abs(z)))
    log2_a_scale = (-LRU_C * LOG2E) * softplus
    gate_a_b = ba_ref[...] * (-LOG2E)
    gate_x_b = bx_ref[...] * (-LOG2E)

    shifts = [s for s in (1, 2, 4) if s < sub]
    valid = [row >= s for s in shifts]
    h_last = hc_ref[...]
    for g0 in range(0, tt, sub):
        rows = slice(g0, g0 + sub)
        xc = xc_ref[rows, :]
        r = _sigmoid_of_scaled(za_ref[rows, :] + gate_a_b)
        i = _sigmoid_of_scaled(zx_ref[rows, :] + gate_x_b)
        log2_a = log2_a_scale * r
        a = jnp.exp2(log2_a)
        b = _sqrt_nonneg(_one_minus_square(log2_a)) * (i * xc)
        for s, ok in zip(shifts, valid):
            a_prev = jnp.where(ok, pltpu.roll(a, s, 0), 1.0)
            b_prev = jnp.where(ok, pltpu.roll(b, s, 0), 0.0)
            b = a * b_prev + b
            a = a * a_prev
        h = a * h_last + b
        h_ref[rows, :] = h
        h_last = h[sub - 1:sub]
    hc_ref[...] = h_last

    g = g_ref[...].astype(F32)
    o_ref[...] = (h_ref[...] * (g * _sigmoid_of_scaled(g * (-LOG2E)))).astype(o_ref.dtype)


def _rglru(proj, conv_w, conv_b, wa_bf16, ba, wx_bf16, bx, log_a_param):
    batch, seq, n_proj = proj.shape
    c = n_proj // 2
    tt = LRU_TIME_TILE
    blk = c // LRU_BLOCKS
    assert seq % tt == 0 and tt % V7X_SUBLANES == 0

    def row(n):
        return pl.BlockSpec((1, n), lambda b, t: (0, 0))

    w_spec = pl.BlockSpec((LRU_BLOCKS, blk, blk), lambda b, t: (0, 0, 0))
    blocks = 3 * tt * c * 2 + 2 * LRU_BLOCKS * blk * blk * 2 + 8 * c * 4
    temps = 14 * tt * c * 4
    return pl.pallas_call(
        _rglru_kernel,
        grid=(batch, seq // tt),
        in_specs=[
            pl.BlockSpec((None, tt, c), lambda b, t: (b, t, 0)),
            pl.BlockSpec((None, tt, c), lambda b, t: (b, t, 1)),
            pl.BlockSpec((CONV_WIDTH, c), lambda b, t: (0, 0)),
            row(c), w_spec, row(c), w_spec, row(c), row(c),
        ],
        out_specs=pl.BlockSpec((None, tt, c), lambda b, t: (b, t, 0)),
        out_shape=jax.ShapeDtypeStruct((batch, seq, c), BF16),
        scratch_shapes=[
            pltpu.VMEM((V7X_SUBLANES, c), F32),
            pltpu.VMEM((1, c), F32),
            pltpu.VMEM((tt, c), F32),
            pltpu.VMEM((tt, c), F32),
            pltpu.VMEM((tt, c), F32),
            pltpu.VMEM((tt, c), F32),
        ],
        compiler_params=pltpu.CompilerParams(
            dimension_semantics=("arbitrary", "arbitrary"),
            vmem_limit_bytes=_vmem_limit(blocks, temps)),
        name="rglru",
    )(proj, proj, conv_w, conv_b.reshape(1, c), wa_bf16, ba.reshape(1, c), wx_bf16,
      bx.reshape(1, c), log_a_param.reshape(1, c))


def kernel(x, pre_norm_w, post_norm_w, attn_w_in, attn_w_out, attn_lambda_q1, attn_lambda_k1,
           attn_lambda_q2, attn_lambda_k2, attn_subln_w, lru_w_in, lru_conv_w, lru_conv_b,
           lru_gate_a_w, lru_gate_a_b, lru_gate_x_w, lru_gate_x_b, lru_log_a_param, lru_w_out):
    batch, seq, d = x.shape
    depth = pre_norm_w.shape[0]
    x2d = x.reshape(batch * seq, d)
    for layer in range(depth):
        j = layer // N_MIXERS
        if layer % N_MIXERS == 0:
            width = attn_w_out.shape[1]
            lam_init = 0.8 - 0.6 * math.exp(-0.3 * layer)
            proj = _norm_proj(x2d, pre_norm_w[layer], attn_w_in[j].astype(BF16),
                              scaled_cols=width, col_scale=ATTN_HEAD_DIM ** -0.5 * LOG2E)
            y = _diff_attention(proj.reshape(batch, seq, 4 * width), attn_lambda_q1[j],
                                attn_lambda_k1[j], attn_lambda_q2[j], attn_lambda_k2[j],
                                attn_subln_w[j], lam_init)
            w_out = attn_w_out[j]
        else:
            width = lru_w_out.shape[1]
            proj = _norm_proj(x2d, pre_norm_w[layer], lru_w_in[j].astype(BF16))
            y = _rglru(proj.reshape(batch, seq, 2 * width), lru_conv_w[j], lru_conv_b[j],
                       (lru_gate_a_w[j] * (-LOG2E)).astype(BF16), lru_gate_a_b[j],
                       (lru_gate_x_w[j] * (-LOG2E)).astype(BF16), lru_gate_x_b[j],
                       lru_log_a_param[j])
            w_out = lru_w_out[j]
        x2d = _out_proj(y.reshape(batch * seq, width), w_out.astype(BF16), post_norm_w[layer], x2d)
    return x2d.reshape(batch, seq, d)
```

```python
import functools
import math

import jax
import jax.numpy as jnp
from jax import lax
from jax.experimental import pallas as pl
from jax.experimental.pallas import tpu as pltpu

F32 = jnp.float32
BF16 = jnp.bfloat16

N_MIXERS = 2
ATTN_HEADS = 8
ATTN_HEAD_DIM = 64
HEAD_WIDTH = 2 * ATTN_HEAD_DIM
LRU_BLOCKS = 4
CONV_WIDTH = 4
LRU_C = 8.0
NORM_EPS = 1e-6
LOG2E = math.log2(math.e)
MASKED_SCORE = -1e30

V7X_VMEM_BYTES = 64 * 1024 * 1024
V7X_SUBLANES = 8
V7X_LANES = 128

TOKEN_TILE = 512
PROJ_N_CHUNK = 512
ATTN_Q_TILE = 256
ATTN_KEY_SPAN = 512
LRU_TIME_TILE = 256


def _vmem_limit(block_bytes, temp_bytes):
    return int(min(V7X_VMEM_BYTES - (4 << 20), 2 * block_bytes + temp_bytes + (4 << 20)))


def _rms_scale(x):
    return lax.rsqrt(jnp.mean(x * x, axis=-1, keepdims=True) + NORM_EPS)


def _norm_proj_body(x, nw_ref, w_ref, o_ref, scaled_cols, col_scale):
    xn = ((x * _rms_scale(x)) * nw_ref[...]).astype(BF16)
    n_out = o_ref.shape[-1]
    for n0 in range(0, n_out, PROJ_N_CHUNK):
        acc = jnp.dot(xn, w_ref[:, n0:n0 + PROJ_N_CHUNK], preferred_element_type=F32)
        if n0 < scaled_cols:
            acc = acc * col_scale
        o_ref[:, n0:n0 + PROJ_N_CHUNK] = acc.astype(o_ref.dtype)


def _norm_proj_kernel(x_ref, nw_ref, w_ref, o_ref, *, scaled_cols, col_scale):
    _norm_proj_body(x_ref[...], nw_ref, w_ref, o_ref, scaled_cols, col_scale)


def _norm_proj(x2d, norm_w, w_bf16, *, scaled_cols=0, col_scale=1.0):
    tokens, d = x2d.shape
    n_out = w_bf16.shape[1]
    assert tokens % TOKEN_TILE == 0 and n_out % PROJ_N_CHUNK == 0
    assert scaled_cols % PROJ_N_CHUNK == 0
    blocks = TOKEN_TILE * d * 4 + d * 4 + d * n_out * 2 + TOKEN_TILE * n_out * 2
    temps = TOKEN_TILE * d * 6 + 2 * TOKEN_TILE * PROJ_N_CHUNK * 4
    return pl.pallas_call(
        functools.partial(_norm_proj_kernel, scaled_cols=scaled_cols, col_scale=col_scale),
        grid=(tokens // TOKEN_TILE,),
        in_specs=[
            pl.BlockSpec((TOKEN_TILE, d), lambda i: (i, 0)),
            pl.BlockSpec((1, d), lambda i: (0, 0)),
            pl.BlockSpec((d, n_out), lambda i: (0, 0)),
        ],
        out_specs=pl.BlockSpec((TOKEN_TILE, n_out), lambda i: (i, 0)),
        out_shape=jax.ShapeDtypeStruct((tokens, n_out), BF16),
        compiler_params=pltpu.CompilerParams(
            dimension_semantics=("arbitrary",),
            vmem_limit_bytes=_vmem_limit(blocks, temps)),
        name="norm_proj",
    )(x2d, norm_w.reshape(1, d), w_bf16)


def _out_proj_kernel(y_ref, w_ref, pw_ref, x_ref, o_ref):
    out = jnp.dot(y_ref[...], w_ref[...], preferred_element_type=F32)
    o_ref[...] = x_ref[...] + (out * _rms_scale(out)) * pw_ref[...]


def _out_proj(y2d, w_bf16, post_w, x2d):
    tokens, d = x2d.shape
    width = y2d.shape[1]
    blocks = TOKEN_TILE * width * 2 + width * d * 2 + d * 4 + 2 * TOKEN_TILE * d * 4
    temps = 2 * TOKEN_TILE * d * 4
    return pl.pallas_call(
        _out_proj_kernel,
        grid=(tokens // TOKEN_TILE,),
        in_specs=[
            pl.BlockSpec((TOKEN_TILE, width), lambda i: (i, 0)),
            pl.BlockSpec((width, d), lambda i: (0, 0)),
            pl.BlockSpec((1, d), lambda i: (0, 0)),
            pl.BlockSpec((TOKEN_TILE, d), lambda i: (i, 0)),
        ],
        out_specs=pl.BlockSpec((TOKEN_TILE, d), lambda i: (i, 0)),
        out_shape=jax.ShapeDtypeStruct((tokens, d), F32),
        compiler_params=pltpu.CompilerParams(
            dimension_semantics=("arbitrary",),
            vmem_limit_bytes=_vmem_limit(blocks, temps)),
        name="out_proj",
    )(y2d, w_bf16, post_w.reshape(1, d), x2d)


def _out_in_proj_kernel(y_ref, wo_ref, pw_ref, x_ref, nw_ref, wi_ref, xo_ref, po_ref, *,
                        scaled_cols, col_scale):
    out = jnp.dot(y_ref[...], wo_ref[...], preferred_element_type=F32)
    x_new = x_ref[...] + (out * _rms_scale(out)) * pw_ref[...]
    xo_ref[...] = x_new
    _norm_proj_body(x_new, nw_ref, wi_ref, po_ref, scaled_cols, col_scale)


def _out_in_proj(y2d, w_out_bf16, post_w, x2d, norm_w, w_in_bf16, *, scaled_cols=0,
                 col_scale=1.0):
    tokens, d = x2d.shape
    width = y2d.shape[1]
    n_out = w_in_bf16.shape[1]
    assert tokens % TOKEN_TILE == 0 and n_out % PROJ_N_CHUNK == 0
    assert scaled_cols % PROJ_N_CHUNK == 0
    blocks = (TOKEN_TILE * width * 2 + width * d * 2 + 2 * d * 4 + 2 * TOKEN_TILE * d * 4
              + d * n_out * 2 + TOKEN_TILE * n_out * 2)
    temps = TOKEN_TILE * d * 10 + 2 * TOKEN_TILE * PROJ_N_CHUNK * 4
    row = pl.BlockSpec((1, d), lambda i: (0, 0))
    return pl.pallas_call(
        functools.partial(_out_in_proj_kernel, scaled_cols=scaled_cols, col_scale=col_scale),
        grid=(tokens // TOKEN_TILE,),
        in_specs=[
            pl.BlockSpec((TOKEN_TILE, width), lambda i: (i, 0)),
            pl.BlockSpec((width, d), lambda i: (0, 0)),
            row,
            pl.BlockSpec((TOKEN_TILE, d), lambda i: (i, 0)),
            row,
            pl.BlockSpec((d, n_out), lambda i: (0, 0)),
        ],
        out_specs=[pl.BlockSpec((TOKEN_TILE, d), lambda i: (i, 0)),
                   pl.BlockSpec((TOKEN_TILE, n_out), lambda i: (i, 0))],
        out_shape=[jax.ShapeDtypeStruct((tokens, d), F32),
                   jax.ShapeDtypeStruct((tokens, n_out), BF16)],
        compiler_params=pltpu.CompilerParams(
            dimension_semantics=("arbitrary",),
            vmem_limit_bytes=_vmem_limit(blocks, temps)),
        name="out_in_proj",
    )(y2d, w_out_bf16, post_w.reshape(1, d), x2d, norm_w.reshape(1, d), w_in_bf16)


def _diff_attn_kernel(lq1_ref, lk1_ref, lq2_ref, lk2_ref, sw_ref, q_ref, k_ref, v_ref, g_ref,
                      o_ref, bias_ref, vt_ref, qt_ref, *, lam_init, n_heads):
    seq = q_ref.shape[0]
    tq = ATTN_Q_TILE
    n_tiles = seq // tq
    head = pl.program_id(1)

    lam = (jnp.exp(jnp.sum(lq1_ref[...] * lk1_ref[...], keepdims=True))
           - jnp.exp(jnp.sum(lq2_ref[...] * lk2_ref[...], keepdims=True)) + lam_init)

    @pl.when(pl.program_id(0) == 0)
    def _():
        head_p1 = (head + 1).astype(F32)
        slope = jnp.exp2(jnp.full((seq, HEAD_WIDTH), -(8.0 / n_heads), F32) * head_p1) * LOG2E
        key_pos = lax.broadcasted_iota(jnp.int32, (seq, HEAD_WIDTH), 0).astype(F32)
        key_lane = lax.broadcasted_iota(jnp.int32, (seq, HEAD_WIDTH), 1)
        bias = slope * key_pos
        bias_hi = bias.astype(BF16).astype(F32)
        rest = bias - bias_hi
        bias_mid = rest.astype(BF16).astype(F32)
        bias_lo = rest - bias_mid
        pieces = jnp.where(key_lane == 0, bias_hi,
                           jnp.where(key_lane == 1, bias_mid,
                                     jnp.where(key_lane == 2, bias_lo, 0.0)))
        bias_ref[head] = pieces.astype(BF16)

    vt_ref[0:HEAD_WIDTH, :] = v_ref[...].T
    vt_ref[HEAD_WIDTH:, :] = jnp.ones((vt_ref.shape[0] - HEAD_WIDTH, seq), BF16)

    qt_ref[...] = q_ref[...].T

    feature = lax.broadcasted_iota(jnp.int32, (HEAD_WIDTH, tq), 0)
    ones_rows = jnp.where(lax.broadcasted_iota(jnp.int32, (HEAD_WIDTH, 2 * tq), 0) < 3,
                          1.0, 0.0).astype(BF16)
    out_scale = sw_ref[...] * (1.0 - lam_init)

    def stacked_queries(qi):
        q_t = qt_ref[:, qi * tq:(qi + 1) * tq]
        zero = jnp.zeros_like(q_t)
        top = jnp.concatenate([jnp.where(feature < ATTN_HEAD_DIM, q_t, zero),
                               jnp.where(feature >= ATTN_HEAD_DIM, q_t, zero)], axis=1)
        return jnp.concatenate([top, ones_rows], axis=0)

    def scores(q_stacked, keys_at):
        keys = jnp.concatenate([k_ref[keys_at, :], bias_ref[head, keys_at, :]], axis=1)
        return jnp.dot(keys, q_stacked, preferred_element_type=F32)

    tiles = []
    for qi in range(n_tiles):
        k0 = 0
        while k0 < qi * tq:
            k1 = min(k0 + ATTN_KEY_SPAN, qi * tq)
            tiles.append((qi, slice(k0, k1), k0 == 0, False))
            k0 = k1
        tiles.append((qi, slice(qi * tq, (qi + 1) * tq), qi == 0, True))

    q_stacked = stacked_queries(0)
    s_next = scores(q_stacked, tiles[0][1])
    for t, (qi, keys_at, first, diagonal) in enumerate(tiles):
        s = s_next
        if t + 1 < len(tiles):
            if tiles[t + 1][0] != qi:
                q_stacked = stacked_queries(tiles[t + 1][0])
            s_next = scores(q_stacked, tiles[t + 1][1])
        if diagonal:
            key_row = lax.broadcasted_iota(jnp.int32, (tq, 2 * tq), 0)
            query_col = lax.broadcasted_iota(jnp.int32, (tq, 2 * tq), 1) & (tq - 1)
            s = jnp.where(key_row <= query_col, s, MASKED_SCORE)
        values_t = vt_ref[:, keys_at]
        if first:
            m = jnp.max(s, axis=0, keepdims=True)
            acc = jnp.dot(values_t, jnp.exp2(s - m).astype(BF16),
                          preferred_element_type=F32)
        else:
            m_new = jnp.maximum(m, jnp.max(s, axis=0, keepdims=True))
            acc = jnp.exp2(m - m_new) * acc + jnp.dot(
                values_t, jnp.exp2(s - m_new).astype(BF16), preferred_element_type=F32)
            m = m_new
        if not diagonal:
            continue

        num = acc[:HEAD_WIDTH]
        inv_den = 1.0 / acc[HEAD_WIDTH:HEAD_WIDTH + 1]
        o_t = num[:, :tq] * inv_den[:, :tq] - num[:, tq:] * (lam * inv_den[:, tq:])
        o_t = o_t * lax.rsqrt(jnp.mean(o_t * o_t, axis=0, keepdims=True) + NORM_EPS)
        g = g_ref[qi * tq:(qi + 1) * tq, :].astype(F32)
        silu_g = g * _sigmoid_of_scaled(g * (-LOG2E))
        o_ref[qi * tq:(qi + 1) * tq, :] = ((o_t.T * out_scale) * silu_g).astype(o_ref.dtype)


def _diff_attention(proj, lq1, lk1, lq2, lk2, subln_w, lam_init):
    batch, seq, n_proj = proj.shape
    width = n_proj // 4
    n_heads = width // HEAD_WIDTH
    assert seq % ATTN_Q_TILE == 0
    d = lq1.shape[-1]

    def head_block(section):
        return pl.BlockSpec((None, seq, HEAD_WIDTH),
                            lambda b, h, s=section: (b, 0, s * n_heads + h))

    def small(n):
        return pl.BlockSpec((1, n), lambda b, h: (0, 0))

    blocks = 5 * seq * HEAD_WIDTH * 2
    temps = 3 * seq * HEAD_WIDTH * 2 + 32 * (2 * ATTN_Q_TILE) * ATTN_Q_TILE * 4
    return pl.pallas_call(
        functools.partial(_diff_attn_kernel, lam_init=lam_init, n_heads=n_heads),
        grid=(batch, n_heads),
        in_specs=[small(d), small(d), small(d), small(d), small(HEAD_WIDTH),
                  head_block(0), head_block(1), head_block(2), head_block(3)],
        out_specs=pl.BlockSpec((None, seq, HEAD_WIDTH), lambda b, h: (b, 0, h)),
        out_shape=jax.ShapeDtypeStruct((batch, seq, width), BF16),
        scratch_shapes=[
            pltpu.VMEM((n_heads, seq, HEAD_WIDTH), BF16),
            pltpu.VMEM((HEAD_WIDTH + 2 * V7X_SUBLANES, seq), BF16),
            pltpu.VMEM((HEAD_WIDTH, seq), BF16),
        ],
        compiler_params=pltpu.CompilerParams(
            dimension_semantics=("arbitrary", "arbitrary"),
            vmem_limit_bytes=_vmem_limit(blocks, temps)),
        name="diff_attn",
    )(lq1.reshape(1, d), lk1.reshape(1, d), lq2.reshape(1, d), lk2.reshape(1, d),
      subln_w.reshape(1, HEAD_WIDTH), proj, proj, proj, proj)


def _sigmoid_of_scaled(x_scaled):
    return 1.0 / (1.0 + jnp.exp2(x_scaled))


def _one_minus_square(log2_a):
    t = jnp.tanh(log2_a * (1.0 / LOG2E))
    return (-2.0 * t) / (1.0 - t)


def _sqrt_nonneg(x):
    positive = x > 0.0
    return jnp.where(positive, x * lax.rsqrt(jnp.where(positive, x, 1.0)), 0.0)


def _rglru_kernel(xr_ref, g_ref, cw_ref, cb_ref, wa_ref, ba_ref, wx_ref, bx_ref, la_ref,
                  o_ref, halo_ref, hc_ref, xc_ref, za_ref, zx_ref, h_ref):
    tt, c = xr_ref.shape
    sub = V7X_SUBLANES
    blk = c // LRU_BLOCKS

    @pl.when(pl.program_id(1) == 0)
    def _():
        halo_ref[...] = jnp.zeros_like(halo_ref)
        hc_ref[...] = jnp.zeros_like(hc_ref)

    row = lax.broadcasted_iota(jnp.int32, (sub, c), 0)

    x_all = xr_ref[...].astype(F32)
    taps = [cw_ref[j:j + 1, :] for j in range(CONV_WIDTH)]
    conv_b = cb_ref[...]
    delays = range(1, CONV_WIDTH)
    prev_rolled = [pltpu.roll(halo_ref[...], d, 0) for d in delays]
    for g0 in range(0, tt, sub):
        x = x_all[g0:g0 + sub]
        rolled = [pltpu.roll(x, d, 0) for d in delays]
        xc = conv_b + taps[CONV_WIDTH - 1] * x
        for d in delays:
            delayed = jnp.where(row >= d, rolled[d - 1], prev_rolled[d - 1])
            xc = xc + taps[CONV_WIDTH - 1 - d] * delayed
        xc_ref[g0:g0 + sub, :] = xc
        prev_rolled = rolled
    halo_ref[...] = x_all[tt - sub:tt]

    xc_bf = xc_ref[...].astype(BF16)
    for blk_i in range(LRU_BLOCKS):
        cols = slice(blk_i * blk, (blk_i + 1) * blk)
        za_ref[:, cols] = jnp.dot(xc_bf[:, cols], wa_ref[blk_i], preferred_element_type=F32)
        zx_ref[:, cols] = jnp.dot(xc_bf[:, cols], wx_ref[blk_i], preferred_element_type=F32)

    z = -la_ref[...]
    softplus = jnp.maximum(z, 0.0) + jnp.log1p(jnp.exp(-jnp.abs(z)))
    log2_a_scale = (-LRU_C * LOG2E) * softplus
    gate_a_b = ba_ref[...] * (-LOG2E)
    gate_x_b = bx_ref[...] * (-LOG2E)

    shifts = [s for s in (1, 2, 4) if s < sub]
    valid = [row >= s for s in shifts]
    h_last = hc_ref[...]
    for g0 in range(0, tt, sub):
        rows = slice(g0, g0 + sub)
        xc = xc_ref[rows, :]
        r = _sigmoid_of_scaled(za_ref[rows, :] + gate_a_b)
        i = _sigmoid_of_scaled(zx_ref[rows, :] + gate_x_b)
        log2_a = log2_a_scale * r
        a = jnp.exp2(log2_a)
        b = _sqrt_nonneg(_one_minus_square(log2_a)) * (i * xc)
        for s, ok in zip(shifts, valid):
            a_prev = jnp.where(ok, pltpu.roll(a, s, 0), 1.0)
            b_prev = jnp.where(ok, pltpu.roll(b, s, 0), 0.0)
            b = a * b_prev + b
            a = a * a_prev
        h = a * h_last + b
        h_ref[rows, :] = h
        h_last = h[sub - 1:sub]
    hc_ref[...] = h_last

    g = g_ref[...].astype(F32)
    o_ref[...] = (h_ref[...] * (g * _sigmoid_of_scaled(g * (-LOG2E)))).astype(o_ref.dtype)


def _rglru(proj, conv_w, conv_b, wa_bf16, ba, wx_bf16, bx, log_a_param):
    batch, seq, n_proj = proj.shape
    c = n_proj // 2
    tt = LRU_TIME_TILE
    blk = c // LRU_BLOCKS
    assert seq % tt == 0 and tt % V7X_SUBLANES == 0

    def row(n):
        return pl.BlockSpec((1, n), lambda b, t: (0, 0))

    w_spec = pl.BlockSpec((LRU_BLOCKS, blk, blk), lambda b, t: (0, 0, 0))
    blocks = 3 * tt * c * 2 + 2 * LRU_BLOCKS * blk * blk * 2 + 8 * c * 4
    temps = 14 * tt * c * 4
    return pl.pallas_call(
        _rglru_kernel,
        grid=(batch, seq // tt),
        in_specs=[
            pl.BlockSpec((None, tt, c), lambda b, t: (b, t, 0)),
            pl.BlockSpec((None, tt, c), lambda b, t: (b, t, 1)),
            pl.BlockSpec((CONV_WIDTH, c), lambda b, t: (0, 0)),
            row(c), w_spec, row(c), w_spec, row(c), row(c),
        ],
        out_specs=pl.BlockSpec((None, tt, c), lambda b, t: (b, t, 0)),
        out_shape=jax.ShapeDtypeStruct((batch, seq, c), BF16),
        scratch_shapes=[
            pltpu.VMEM((V7X_SUBLANES, c), F32),
            pltpu.VMEM((1, c), F32),
            pltpu.VMEM((tt, c), F32),
            pltpu.VMEM((tt, c), F32),
            pltpu.VMEM((tt, c), F32),
            pltpu.VMEM((tt, c), F32),
        ],
        compiler_params=pltpu.CompilerParams(
            dimension_semantics=("arbitrary", "arbitrary"),
            vmem_limit_bytes=_vmem_limit(blocks, temps)),
        name="rglru",
    )(proj, proj, conv_w, conv_b.reshape(1, c), wa_bf16, ba.reshape(1, c), wx_bf16,
      bx.reshape(1, c), log_a_param.reshape(1, c))


def kernel(x, pre_norm_w, post_norm_w, attn_w_in, attn_w_out, attn_lambda_q1, attn_lambda_k1,
           attn_lambda_q2, attn_lambda_k2, attn_subln_w, lru_w_in, lru_conv_w, lru_conv_b,
           lru_gate_a_w, lru_gate_a_b, lru_gate_x_w, lru_gate_x_b, lru_log_a_param, lru_w_out):
    batch, seq, d = x.shape
    depth = pre_norm_w.shape[0]
    x2d = x.reshape(batch * seq, d)

    def in_proj_args(layer):
        j = layer // N_MIXERS
        if layer % N_MIXERS == 0:
            return attn_w_in[j].astype(BF16), dict(
                scaled_cols=attn_w_out.shape[1], col_scale=ATTN_HEAD_DIM ** -0.5 * LOG2E)
        return lru_w_in[j].astype(BF16), {}

    w_in, in_kwargs = in_proj_args(0)
    proj = _norm_proj(x2d, pre_norm_w[0], w_in, **in_kwargs)
    for layer in range(depth):
        j = layer // N_MIXERS
        if layer % N_MIXERS == 0:
            width = attn_w_out.shape[1]
            lam_init = 0.8 - 0.6 * math.exp(-0.3 * layer)
            y = _diff_attention(proj.reshape(batch, seq, 4 * width), attn_lambda_q1[j],
                                attn_lambda_k1[j], attn_lambda_q2[j], attn_lambda_k2[j],
                                attn_subln_w[j], lam_init)
            w_out = attn_w_out[j]
        else:
            width = lru_w_out.shape[1]
            y = _rglru(proj.reshape(batch, seq, 2 * width), lru_conv_w[j], lru_conv_b[j],
                       (lru_gate_a_w[j] * (-LOG2E)).astype(BF16), lru_gate_a_b[j],
                       (lru_gate_x_w[j] * (-LOG2E)).astype(BF16), lru_gate_x_b[j],
                       lru_log_a_param[j])
            w_out = lru_w_out[j]
        y2d = y.reshape(batch * seq, width)
        if layer + 1 < depth:
            w_in, in_kwargs = in_proj_args(layer + 1)
            x2d, proj = _out_in_proj(y2d, w_out.astype(BF16), post_norm_w[layer], x2d,
                                     pre_norm_w[layer + 1], w_in, **in_kwargs)
        else:
            x2d = _out_proj(y2d, w_out.astype(BF16), post_norm_w[layer], x2d)
    return x2d.reshape(batch, seq, d)
```

```python
import functools
import math

import jax
import jax.numpy as jnp
from jax import lax
from jax.experimental import pallas as pl
from jax.experimental.pallas import tpu as pltpu

F32 = jnp.float32
BF16 = jnp.bfloat16

N_MIXERS = 2
ATTN_HEADS = 8
ATTN_HEAD_DIM = 64
HEAD_WIDTH = 2 * ATTN_HEAD_DIM
LRU_BLOCKS = 4
CONV_WIDTH = 4
LRU_C = 8.0
NORM_EPS = 1e-6
LOG2E = math.log2(math.e)
MASKED_SCORE = -1e30

V7X_VMEM_BYTES = 64 * 1024 * 1024
V7X_SUBLANES = 8
V7X_LANES = 128

TOKEN_TILE = 1024
PROJ_N_CHUNK = 512
ATTN_Q_TILE = 256
ATTN_KEY_SPAN = 512
LRU_TIME_TILE = 256


def _vmem_limit(block_bytes, temp_bytes):
    return int(min(V7X_VMEM_BYTES - (4 << 20), 2 * block_bytes + temp_bytes + (4 << 20)))


def _rms_scale(x):
    return lax.rsqrt(jnp.mean(x * x, axis=-1, keepdims=True) + NORM_EPS)


def _norm_proj_body(x, nw_ref, w_ref, o_ref, scaled_cols, col_scale):
    xn = ((x * _rms_scale(x)) * nw_ref[...]).astype(BF16)
    n_out = o_ref.shape[-1]
    for n0 in range(0, n_out, PROJ_N_CHUNK):
        acc = jnp.dot(xn, w_ref[:, n0:n0 + PROJ_N_CHUNK], preferred_element_type=F32)
        if n0 < scaled_cols:
            acc = acc * col_scale
        o_ref[:, n0:n0 + PROJ_N_CHUNK] = acc.astype(o_ref.dtype)


def _norm_proj_kernel(x_ref, nw_ref, w_ref, o_ref, *, scaled_cols, col_scale):
    _norm_proj_body(x_ref[...], nw_ref, w_ref, o_ref, scaled_cols, col_scale)


def _norm_proj(x2d, norm_w, w_bf16, *, scaled_cols=0, col_scale=1.0):
    tokens, d = x2d.shape
    n_out = w_bf16.shape[1]
    assert tokens % TOKEN_TILE == 0 and n_out % PROJ_N_CHUNK == 0
    assert scaled_cols % PROJ_N_CHUNK == 0
    blocks = TOKEN_TILE * d * 4 + d * 4 + d * n_out * 2 + TOKEN_TILE * n_out * 2
    temps = TOKEN_TILE * d * 6 + 2 * TOKEN_TILE * PROJ_N_CHUNK * 4
    return pl.pallas_call(
        functools.partial(_norm_proj_kernel, scaled_cols=scaled_cols, col_scale=col_scale),
        grid=(tokens // TOKEN_TILE,),
        in_specs=[
            pl.BlockSpec((TOKEN_TILE, d), lambda i: (i, 0)),
            pl.BlockSpec((1, d), lambda i: (0, 0)),
            pl.BlockSpec((d, n_out), lambda i: (0, 0)),
        ],
        out_specs=pl.BlockSpec((TOKEN_TILE, n_out), lambda i: (i, 0)),
        out_shape=jax.ShapeDtypeStruct((tokens, n_out), BF16),
        compiler_params=pltpu.CompilerParams(
            dimension_semantics=("arbitrary",),
            vmem_limit_bytes=_vmem_limit(blocks, temps)),
        name="norm_proj",
    )(x2d, norm_w.reshape(1, d), w_bf16)


def _out_proj_kernel(y_ref, w_ref, pw_ref, x_ref, o_ref):
    out = jnp.dot(y_ref[...], w_ref[...], preferred_element_type=F32)
    o_ref[...] = x_ref[...] + (out * _rms_scale(out)) * pw_ref[...]


def _out_proj(y2d, w_bf16, post_w, x2d):
    tokens, d = x2d.shape
    width = y2d.shape[1]
    blocks = TOKEN_TILE * width * 2 + width * d * 2 + d * 4 + 2 * TOKEN_TILE * d * 4
    temps = 2 * TOKEN_TILE * d * 4
    return pl.pallas_call(
        _out_proj_kernel,
        grid=(tokens // TOKEN_TILE,),
        in_specs=[
            pl.BlockSpec((TOKEN_TILE, width), lambda i: (i, 0)),
            pl.BlockSpec((width, d), lambda i: (0, 0)),
            pl.BlockSpec((1, d), lambda i: (0, 0)),
            pl.BlockSpec((TOKEN_TILE, d), lambda i: (i, 0)),
        ],
        out_specs=pl.BlockSpec((TOKEN_TILE, d), lambda i: (i, 0)),
        out_shape=jax.ShapeDtypeStruct((tokens, d), F32),
        compiler_params=pltpu.CompilerParams(
            dimension_semantics=("arbitrary",),
            vmem_limit_bytes=_vmem_limit(blocks, temps)),
        name="out_proj",
    )(y2d, w_bf16, post_w.reshape(1, d), x2d)


def _out_in_proj_kernel(y_ref, wo_ref, pw_ref, x_ref, nw_ref, wi_ref, xo_ref, po_ref, *,
                        scaled_cols, col_scale):
    out = jnp.dot(y_ref[...], wo_ref[...], preferred_element_type=F32)
    x_new = x_ref[...] + (out * _rms_scale(out)) * pw_ref[...]
    xo_ref[...] = x_new
    _norm_proj_body(x_new, nw_ref, wi_ref, po_ref, scaled_cols, col_scale)


def _out_in_proj(y2d, w_out_bf16, post_w, x2d, norm_w, w_in_bf16, *, scaled_cols=0,
                 col_scale=1.0):
    tokens, d = x2d.shape
    width = y2d.shape[1]
    n_out = w_in_bf16.shape[1]
    assert tokens % TOKEN_TILE == 0 and n_out % PROJ_N_CHUNK == 0
    assert scaled_cols % PROJ_N_CHUNK == 0
    blocks = (TOKEN_TILE * width * 2 + width * d * 2 + 2 * d * 4 + 2 * TOKEN_TILE * d * 4
              + d * n_out * 2 + TOKEN_TILE * n_out * 2)
    temps = TOKEN_TILE * d * 10 + 2 * TOKEN_TILE * PROJ_N_CHUNK * 4
    row = pl.BlockSpec((1, d), lambda i: (0, 0))
    return pl.pallas_call(
        functools.partial(_out_in_proj_kernel, scaled_cols=scaled_cols, col_scale=col_scale),
        grid=(tokens // TOKEN_TILE,),
        in_specs=[
            pl.BlockSpec((TOKEN_TILE, width), lambda i: (i, 0)),
            pl.BlockSpec((width, d), lambda i: (0, 0)),
            row,
            pl.BlockSpec((TOKEN_TILE, d), lambda i: (i, 0)),
            row,
            pl.BlockSpec((d, n_out), lambda i: (0, 0)),
        ],
        out_specs=[pl.BlockSpec((TOKEN_TILE, d), lambda i: (i, 0)),
                   pl.BlockSpec((TOKEN_TILE, n_out), lambda i: (i, 0))],
        out_shape=[jax.ShapeDtypeStruct((tokens, d), F32),
                   jax.ShapeDtypeStruct((tokens, n_out), BF16)],
        compiler_params=pltpu.CompilerParams(
            dimension_semantics=("arbitrary",),
            vmem_limit_bytes=_vmem_limit(blocks, temps)),
        name="out_in_proj",
    )(y2d, w_out_bf16, post_w.reshape(1, d), x2d, norm_w.reshape(1, d), w_in_bf16)


def _diff_attn_kernel(lq1_ref, lk1_ref, lq2_ref, lk2_ref, sw_ref, q_ref, k_ref, v_ref, g_ref,
                      o_ref, bias_ref, vt_ref, qs_ref, *, lam_init, n_heads):
    seq = q_ref.shape[0]
    tq = ATTN_Q_TILE
    n_tiles = seq // tq
    head = pl.program_id(1)

    @pl.when(pl.program_id(0) == 0)
    def _():
        head_p1 = (head + 1).astype(F32)
        slope = jnp.exp2(jnp.full((seq, HEAD_WIDTH), -(8.0 / n_heads), F32) * head_p1) * LOG2E
        key_pos = lax.broadcasted_iota(jnp.int32, (seq, HEAD_WIDTH), 0).astype(F32)
        key_lane = lax.broadcasted_iota(jnp.int32, (seq, HEAD_WIDTH), 1)
        bias = slope * key_pos
        bias_hi = bias.astype(BF16).astype(F32)
        rest = bias - bias_hi
        bias_mid = rest.astype(BF16).astype(F32)
        bias_lo = rest - bias_mid
        pieces = jnp.where(key_lane == 0, bias_hi,
                           jnp.where(key_lane == 1, bias_mid,
                                     jnp.where(key_lane == 2, bias_lo, 0.0)))
        bias_ref[head] = pieces.astype(BF16)

    vt_ref[HEAD_WIDTH:, :] = jnp.ones((vt_ref.shape[0] - HEAD_WIDTH, seq), BF16)
    lam = (jnp.exp(jnp.sum(lq1_ref[...] * lk1_ref[...], keepdims=True))
           - jnp.exp(jnp.sum(lq2_ref[...] * lk2_ref[...], keepdims=True)) + lam_init)

    feature = lax.broadcasted_iota(jnp.int32, (HEAD_WIDTH, tq), 0)
    ones_rows = jnp.where(lax.broadcasted_iota(jnp.int32, (HEAD_WIDTH, 2 * tq), 0) < 3,
                          1.0, 0.0).astype(BF16)
    out_scale = sw_ref[...] * (1.0 - lam_init)

    def stacked_queries(qi):
        q_t = q_ref[qi * tq:(qi + 1) * tq, :].T
        zero = jnp.zeros_like(q_t)
        top = jnp.concatenate([jnp.where(feature < ATTN_HEAD_DIM, q_t, zero),
                               jnp.where(feature >= ATTN_HEAD_DIM, q_t, zero)], axis=1)
        return jnp.concatenate([top, ones_rows], axis=0)

    def scores(q_stacked, keys_at):
        keys = jnp.concatenate([k_ref[keys_at, :], bias_ref[head, keys_at, :]], axis=1)
        return jnp.dot(keys, q_stacked, preferred_element_type=F32)

    per_query_tile = []
    for qi in range(n_tiles):
        spans = []
        k0 = 0
        while k0 < qi * tq:
            k1 = min(k0 + ATTN_KEY_SPAN, qi * tq)
            spans.append((qi, slice(k0, k1), k0 == 0, False))
            k0 = k1
        spans.append((qi, slice(qi * tq, (qi + 1) * tq), qi == 0, True))
        per_query_tile.append(spans)
    tiles = [spans[r] for r in range(max(map(len, per_query_tile)))
             for spans in per_query_tile if r < len(spans)]

    def queries_for(qi):
        if qi not in stacked_done:
            qs_ref[qi] = stacked_queries(qi)
            stacked_done.add(qi)
        return qs_ref[qi]

    stacked_done = set()
    values_done = set()
    m_of = {}
    acc_of = {}
    s_next = scores(queries_for(tiles[0][0]), tiles[0][1])
    for t, (qi, keys_at, first, diagonal) in enumerate(tiles):
        s = s_next
        if t + 1 < len(tiles):
            s_next = scores(queries_for(tiles[t + 1][0]), tiles[t + 1][1])
        if diagonal:
            key_row = lax.broadcasted_iota(jnp.int32, (tq, 2 * tq), 0)
            query_col = lax.broadcasted_iota(jnp.int32, (tq, 2 * tq), 1) & (tq - 1)
            s = jnp.where(key_row <= query_col, s, MASKED_SCORE)
        for k0 in range(keys_at.start, keys_at.stop, tq):
            if k0 not in values_done:
                vt_ref[0:HEAD_WIDTH, k0:k0 + tq] = v_ref[k0:k0 + tq, :].T
                values_done.add(k0)
        values_t = vt_ref[:, keys_at]
        if first:
            m = jnp.max(s, axis=0, keepdims=True)
            acc = jnp.dot(values_t, jnp.exp2(s - m).astype(BF16),
                          preferred_element_type=F32)
        else:
            m = jnp.maximum(m_of[qi], jnp.max(s, axis=0, keepdims=True))
            acc = jnp.exp2(m_of[qi] - m) * acc_of[qi] + jnp.dot(
                values_t, jnp.exp2(s - m).astype(BF16), preferred_element_type=F32)
        m_of[qi], acc_of[qi] = m, acc
        if not diagonal:
            continue

        num = acc[:HEAD_WIDTH]
        inv_den = 1.0 / acc[HEAD_WIDTH:HEAD_WIDTH + 1]
        o_t = num[:, :tq] * inv_den[:, :tq] - num[:, tq:] * (lam * inv_den[:, tq:])
        o_t = o_t * lax.rsqrt(jnp.mean(o_t * o_t, axis=0, keepdims=True) + NORM_EPS)
        g = g_ref[qi * tq:(qi + 1) * tq, :].astype(F32)
        silu_g = g * _sigmoid_of_scaled(g * (-LOG2E))
        o_ref[qi * tq:(qi + 1) * tq, :] = ((o_t.T * out_scale) * silu_g).astype(o_ref.dtype)


def _diff_attention(proj, lq1, lk1, lq2, lk2, subln_w, lam_init):
    batch, seq, n_proj = proj.shape
    width = n_proj // 4
    n_heads = width // HEAD_WIDTH
    assert seq % ATTN_Q_TILE == 0
    d = lq1.shape[-1]

    def head_block(section):
        return pl.BlockSpec((None, seq, HEAD_WIDTH),
                            lambda b, h, s=section: (b, 0, s * n_heads + h))

    def small(n):
        return pl.BlockSpec((1, n), lambda b, h: (0, 0))

    blocks = 5 * seq * HEAD_WIDTH * 2
    temps = 3 * seq * HEAD_WIDTH * 2 + 32 * (2 * ATTN_Q_TILE) * ATTN_Q_TILE * 4
    return pl.pallas_call(
        functools.partial(_diff_attn_kernel, lam_init=lam_init, n_heads=n_heads),
        grid=(batch, n_heads),
        in_specs=[small(d), small(d), small(d), small(d), small(HEAD_WIDTH),
                  head_block(0), head_block(1), head_block(2), head_block(3)],
        out_specs=pl.BlockSpec((None, seq, HEAD_WIDTH), lambda b, h: (b, 0, h)),
        out_shape=jax.ShapeDtypeStruct((batch, seq, width), BF16),
        scratch_shapes=[
            pltpu.VMEM((n_heads, seq, HEAD_WIDTH), BF16),
            pltpu.VMEM((HEAD_WIDTH + 2 * V7X_SUBLANES, seq), BF16),
            pltpu.VMEM((seq // ATTN_Q_TILE, 2 * HEAD_WIDTH, 2 * ATTN_Q_TILE), BF16),
        ],
        compiler_params=pltpu.CompilerParams(
            dimension_semantics=("arbitrary", "arbitrary"),
            vmem_limit_bytes=_vmem_limit(blocks, temps)),
        name="diff_attn",
    )(lq1.reshape(1, d), lk1.reshape(1, d), lq2.reshape(1, d), lk2.reshape(1, d),
      subln_w.reshape(1, HEAD_WIDTH), proj, proj, proj, proj)


def _sigmoid_of_scaled(x_scaled):
    return 1.0 / (1.0 + jnp.exp2(x_scaled))


def _one_minus_square(log2_a):
    t = jnp.tanh(log2_a * (1.0 / LOG2E))
    return (-2.0 * t) / (1.0 - t)


def _sqrt_nonneg(x):
    positive = x > 0.0
    return jnp.where(positive, x * lax.rsqrt(jnp.where(positive, x, 1.0)), 0.0)


def _rglru_kernel(xr_ref, g_ref, cw_ref, cb_ref, wa_ref, ba_ref, wx_ref, bx_ref, la_ref,
                  o_ref, halo_ref, hc_ref, xc_ref, za_ref, zx_ref, h_ref):
    tt, c = xr_ref.shape
    sub = V7X_SUBLANES
    blk = c // LRU_BLOCKS

    @pl.when(pl.program_id(1) == 0)
    def _():
        halo_ref[...] = jnp.zeros_like(halo_ref)
        hc_ref[...] = jnp.zeros_like(hc_ref)

    row = lax.broadcasted_iota(jnp.int32, (sub, c), 0)

    x_all = xr_ref[...].astype(F32)
    taps = [cw_ref[j:j + 1, :] for j in range(CONV_WIDTH)]
    conv_b = cb_ref[...]
    delays = range(1, CONV_WIDTH)
    prev_rolled = [pltpu.roll(halo_ref[...], d, 0) for d in delays]
    for g0 in range(0, tt, sub):
        x = x_all[g0:g0 + sub]
        rolled = [pltpu.roll(x, d, 0) for d in delays]
        xc = conv_b + taps[CONV_WIDTH - 1] * x
        for d in delays:
            delayed = jnp.where(row >= d, rolled[d - 1], prev_rolled[d - 1])
            xc = xc + taps[CONV_WIDTH - 1 - d] * delayed
        xc_ref[g0:g0 + sub, :] = xc
        prev_rolled = rolled
    halo_ref[...] = x_all[tt - sub:tt]

    xc_bf = xc_ref[...].astype(BF16)
    for blk_i in range(LRU_BLOCKS):
        cols = slice(blk_i * blk, (blk_i + 1) * blk)
        za_ref[:, cols] = jnp.dot(xc_bf[:, cols], wa_ref[blk_i], preferred_element_type=F32)
        zx_ref[:, cols] = jnp.dot(xc_bf[:, cols], wx_ref[blk_i], preferred_element_type=F32)

    z = -la_ref[...]
    softplus = jnp.maximum(z, 0.0) + jnp.log1p(jnp.exp(-jnp.abs(z)))
    log2_a_scale = (-LRU_C * LOG2E) * softplus
    gate_a_b = ba_ref[...] * (-LOG2E)
    gate_x_b = bx_ref[...] * (-LOG2E)

    shifts = [s for s in (1, 2, 4) if s < sub]
    valid = [row >= s for s in shifts]
    h_last = hc_ref[...]
    for g0 in range(0, tt, sub):
        rows = slice(g0, g0 + sub)
        xc = xc_ref[rows, :]
        r = _sigmoid_of_scaled(za_ref[rows, :] + gate_a_b)
        i = _sigmoid_of_scaled(zx_ref[rows, :] + gate_x_b)
        log2_a = log2_a_scale * r
        a = jnp.exp2(log2_a)
        b = _sqrt_nonneg(_one_minus_square(log2_a)) * (i * xc)
        for s, ok in zip(shifts, valid):
            a_prev = jnp.where(ok, pltpu.roll(a, s, 0), 1.0)
            b_prev = jnp.where(ok, pltpu.roll(b, s, 0), 0.0)
            b = a * b_prev + b
            a = a * a_prev
        h = a * h_last + b
        h_ref[rows, :] = h
        h_last = h[sub - 1:sub]
    hc_ref[...] = h_last

    g = g_ref[...].astype(F32)
    o_ref[...] = (h_ref[...] * (g * _sigmoid_of_scaled(g * (-LOG2E)))).astype(o_ref.dtype)


def _rglru(proj, conv_w, conv_b, wa_bf16, ba, wx_bf16, bx, log_a_param):
    batch, seq, n_proj = proj.shape
    c = n_proj // 2
    tt = LRU_TIME_TILE
    blk = c // LRU_BLOCKS
    assert seq % tt == 0 and tt % V7X_SUBLANES == 0

    def row(n):
        return pl.BlockSpec((1, n), lambda b, t: (0, 0))

    w_spec = pl.BlockSpec((LRU_BLOCKS, blk, blk), lambda b, t: (0, 0, 0))
    blocks = 3 * tt * c * 2 + 2 * LRU_BLOCKS * blk * blk * 2 + 8 * c * 4
    temps = 14 * tt * c * 4
    return pl.pallas_call(
        _rglru_kernel,
        grid=(batch, seq // tt),
        in_specs=[
            pl.BlockSpec((None, tt, c), lambda b, t: (b, t, 0)),
            pl.BlockSpec((None, tt, c), lambda b, t: (b, t, 1)),
            pl.BlockSpec((CONV_WIDTH, c), lambda b, t: (0, 0)),
            row(c), w_spec, row(c), w_spec, row(c), row(c),
        ],
        out_specs=pl.BlockSpec((None, tt, c), lambda b, t: (b, t, 0)),
        out_shape=jax.ShapeDtypeStruct((batch, seq, c), BF16),
        scratch_shapes=[
            pltpu.VMEM((V7X_SUBLANES, c), F32),
            pltpu.VMEM((1, c), F32),
            pltpu.VMEM((tt, c), F32),
            pltpu.VMEM((tt, c), F32),
            pltpu.VMEM((tt, c), F32),
            pltpu.VMEM((tt, c), F32),
        ],
        compiler_params=pltpu.CompilerParams(
            dimension_semantics=("arbitrary", "arbitrary"),
            vmem_limit_bytes=_vmem_limit(blocks, temps)),
        name="rglru",
    )(proj, proj, conv_w, conv_b.reshape(1, c), wa_bf16, ba.reshape(1, c), wx_bf16,
      bx.reshape(1, c), log_a_param.reshape(1, c))


def kernel(x, pre_norm_w, post_norm_w, attn_w_in, attn_w_out, attn_lambda_q1, attn_lambda_k1,
           attn_lambda_q2, attn_lambda_k2, attn_subln_w, lru_w_in, lru_conv_w, lru_conv_b,
           lru_gate_a_w, lru_gate_a_b, lru_gate_x_w, lru_gate_x_b, lru_log_a_param, lru_w_out):
    batch, seq, d = x.shape
    depth = pre_norm_w.shape[0]
    x2d = x.reshape(batch * seq, d)

    def in_proj_args(layer):
        j = layer // N_MIXERS
        if layer % N_MIXERS == 0:
            return attn_w_in[j].astype(BF16), dict(
                scaled_cols=attn_w_out.shape[1], col_scale=ATTN_HEAD_DIM ** -0.5 * LOG2E)
        return lru_w_in[j].astype(BF16), {}

    w_in, in_kwargs = in_proj_args(0)
    proj = _norm_proj(x2d, pre_norm_w[0], w_in, **in_kwargs)
    for layer in range(depth):
        j = layer // N_MIXERS
        if layer % N_MIXERS == 0:
            width = attn_w_out.shape[1]
            lam_init = 0.8 - 0.6 * math.exp(-0.3 * layer)
            y = _diff_attention(proj.reshape(batch, seq, 4 * width), attn_lambda_q1[j],
                                attn_lambda_k1[j], attn_lambda_q2[j], attn_lambda_k2[j],
                                attn_subln_w[j], lam_init)
            w_out = attn_w_out[j]
        else:
            width = lru_w_out.shape[1]
            y = _rglru(proj.reshape(batch, seq, 2 * width), lru_conv_w[j], lru_conv_b[j],
                       (lru_gate_a_w[j] * (-LOG2E)).astype(BF16), lru_gate_a_b[j],
                       (lru_gate_x_w[j] * (-LOG2E)).astype(BF16), lru_gate_x_b[j],
                       lru_log_a_param[j])
            w_out = lru_w_out[j]
        y2d = y.reshape(batch * seq, width)
        if layer + 1 < depth:
            w_in, in_kwargs = in_proj_args(layer + 1)
            x2d, proj = _out_in_proj(y2d, w_out.astype(BF16), post_norm_w[layer], x2d,
                                     pre_norm_w[layer + 1], w_in, **in_kwargs)
        else:
            x2d = _out_proj(y2d, w_out.astype(BF16), post_norm_w[layer], x2d)
    return x2d.reshape(batch, seq, d)
```

```python
import functools
import math

import jax
import jax.numpy as jnp
from jax import lax
from jax.experimental import pallas as pl
from jax.experimental.pallas import tpu as pltpu

F32 = jnp.float32
BF16 = jnp.bfloat16

N_MIXERS = 2
ATTN_HEADS = 8
ATTN_HEAD_DIM = 64
HEAD_WIDTH = 2 * ATTN_HEAD_DIM
LRU_BLOCKS = 4
CONV_WIDTH = 4
LRU_C = 8.0
NORM_EPS = 1e-6
LOG2E = math.log2(math.e)
MASKED_SCORE = -1e30

V7X_VMEM_BYTES = 64 * 1024 * 1024
V7X_SUBLANES = 8
V7X_LANES = 128

TOKEN_TILE = 1024
PROJ_N_CHUNK = 512
ATTN_Q_TILE = 256
ATTN_KEY_SPAN = 512
LRU_TIME_TILE = 512


def _vmem_limit(block_bytes, temp_bytes):
    return int(min(V7X_VMEM_BYTES - (4 << 20), 2 * block_bytes + temp_bytes + (4 << 20)))


def _rms_scale(x):
    return lax.rsqrt(jnp.mean(x * x, axis=-1, keepdims=True) + NORM_EPS)


def _norm_proj_body(x, nw_ref, w_ref, o_ref, scaled_cols, col_scale):
    xn = ((x * _rms_scale(x)) * nw_ref[...]).astype(BF16)
    n_out = o_ref.shape[-1]
    for n0 in range(0, n_out, PROJ_N_CHUNK):
        acc = jnp.dot(xn, w_ref[:, n0:n0 + PROJ_N_CHUNK], preferred_element_type=F32)
        if n0 < scaled_cols:
            acc = acc * col_scale
        o_ref[:, n0:n0 + PROJ_N_CHUNK] = acc.astype(o_ref.dtype)


def _norm_proj_kernel(x_ref, nw_ref, w_ref, o_ref, *, scaled_cols, col_scale):
    _norm_proj_body(x_ref[...], nw_ref, w_ref, o_ref, scaled_cols, col_scale)


def _norm_proj(x2d, norm_w, w_bf16, *, scaled_cols=0, col_scale=1.0):
    tokens, d = x2d.shape
    n_out = w_bf16.shape[1]
    assert tokens % TOKEN_TILE == 0 and n_out % PROJ_N_CHUNK == 0
    assert scaled_cols % PROJ_N_CHUNK == 0
    blocks = TOKEN_TILE * d * 4 + d * 4 + d * n_out * 2 + TOKEN_TILE * n_out * 2
    temps = TOKEN_TILE * d * 6 + 2 * TOKEN_TILE * PROJ_N_CHUNK * 4
    return pl.pallas_call(
        functools.partial(_norm_proj_kernel, scaled_cols=scaled_cols, col_scale=col_scale),
        grid=(tokens // TOKEN_TILE,),
        in_specs=[
            pl.BlockSpec((TOKEN_TILE, d), lambda i: (i, 0)),
            pl.BlockSpec((1, d), lambda i: (0, 0)),
            pl.BlockSpec((d, n_out), lambda i: (0, 0)),
        ],
        out_specs=pl.BlockSpec((TOKEN_TILE, n_out), lambda i: (i, 0)),
        out_shape=jax.ShapeDtypeStruct((tokens, n_out), BF16),
        compiler_params=pltpu.CompilerParams(
            dimension_semantics=("arbitrary",),
            vmem_limit_bytes=_vmem_limit(blocks, temps)),
        name="norm_proj",
    )(x2d, norm_w.reshape(1, d), w_bf16)


def _out_in_proj_kernel(y_ref, wo_ref, pw_ref, x_ref, nw_ref, wi_ref, xo_ref, po_ref, *,
                        scaled_cols, col_scale):
    out = jnp.dot(y_ref[...], wo_ref[...], preferred_element_type=F32)
    x_new = x_ref[...] + (out * _rms_scale(out)) * pw_ref[...]
    xo_ref[...] = x_new
    _norm_proj_body(x_new, nw_ref, wi_ref, po_ref, scaled_cols, col_scale)


def _out_in_proj(y2d, w_out_bf16, post_w, x2d, norm_w, w_in_bf16, *, scaled_cols=0,
                 col_scale=1.0):
    tokens, d = x2d.shape
    width = y2d.shape[1]
    n_out = w_in_bf16.shape[1]
    assert tokens % TOKEN_TILE == 0 and n_out % PROJ_N_CHUNK == 0
    assert scaled_cols % PROJ_N_CHUNK == 0
    blocks = (TOKEN_TILE * width * 2 + width * d * 2 + 2 * d * 4 + 2 * TOKEN_TILE * d * 4
              + d * n_out * 2 + TOKEN_TILE * n_out * 2)
    temps = TOKEN_TILE * d * 10 + 2 * TOKEN_TILE * PROJ_N_CHUNK * 4
    row = pl.BlockSpec((1, d), lambda i: (0, 0))
    return pl.pallas_call(
        functools.partial(_out_in_proj_kernel, scaled_cols=scaled_cols, col_scale=col_scale),
        grid=(tokens // TOKEN_TILE,),
        in_specs=[
            pl.BlockSpec((TOKEN_TILE, width), lambda i: (i, 0)),
            pl.BlockSpec((width, d), lambda i: (0, 0)),
            row,
            pl.BlockSpec((TOKEN_TILE, d), lambda i: (i, 0)),
            row,
            pl.BlockSpec((d, n_out), lambda i: (0, 0)),
        ],
        out_specs=[pl.BlockSpec((TOKEN_TILE, d), lambda i: (i, 0)),
                   pl.BlockSpec((TOKEN_TILE, n_out), lambda i: (i, 0))],
        out_shape=[jax.ShapeDtypeStruct((tokens, d), F32),
                   jax.ShapeDtypeStruct((tokens, n_out), BF16)],
        compiler_params=pltpu.CompilerParams(
            dimension_semantics=("arbitrary",),
            vmem_limit_bytes=_vmem_limit(blocks, temps)),
        name="out_in_proj",
    )(y2d, w_out_bf16, post_w.reshape(1, d), x2d, norm_w.reshape(1, d), w_in_bf16)


def _diff_attn_kernel(lq1_ref, lk1_ref, lq2_ref, lk2_ref, sw_ref, q_ref, k_ref, v_ref, g_ref,
                      o_ref, bias_ref, vt_ref, qs_ref, *, lam_init, n_heads):
    seq = q_ref.shape[0]
    tq = ATTN_Q_TILE
    n_tiles = seq // tq
    head = pl.program_id(1)

    @pl.when(pl.program_id(0) == 0)
    def _():
        head_p1 = (head + 1).astype(F32)
        slope = jnp.exp2(jnp.full((seq, HEAD_WIDTH), -(8.0 / n_heads), F32) * head_p1) * LOG2E
        key_pos = lax.broadcasted_iota(jnp.int32, (seq, HEAD_WIDTH), 0).astype(F32)
        key_lane = lax.broadcasted_iota(jnp.int32, (seq, HEAD_WIDTH), 1)
        bias = slope * key_pos
        bias_hi = bias.astype(BF16).astype(F32)
        rest = bias - bias_hi
        bias_mid = rest.astype(BF16).astype(F32)
        bias_lo = rest - bias_mid
        pieces = jnp.where(key_lane == 0, bias_hi,
                           jnp.where(key_lane == 1, bias_mid,
                                     jnp.where(key_lane == 2, bias_lo, 0.0)))
        bias_ref[head] = pieces.astype(BF16)

    vt_ref[HEAD_WIDTH:, :] = jnp.ones((vt_ref.shape[0] - HEAD_WIDTH, seq), BF16)
    lam = (jnp.exp(jnp.sum(lq1_ref[...] * lk1_ref[...], keepdims=True))
           - jnp.exp(jnp.sum(lq2_ref[...] * lk2_ref[...], keepdims=True)) + lam_init)

    feature = lax.broadcasted_iota(jnp.int32, (HEAD_WIDTH, tq), 0)
    ones_rows = jnp.where(lax.broadcasted_iota(jnp.int32, (HEAD_WIDTH, 2 * tq), 0) < 3,
                          1.0, 0.0).astype(BF16)
    out_scale = sw_ref[...] * (1.0 - lam_init)

    def stacked_queries(qi):
        q_t = q_ref[qi * tq:(qi + 1) * tq, :].T
        zero = jnp.zeros_like(q_t)
        top = jnp.concatenate([jnp.where(feature < ATTN_HEAD_DIM, q_t, zero),
                               jnp.where(feature >= ATTN_HEAD_DIM, q_t, zero)], axis=1)
        return jnp.concatenate([top, ones_rows], axis=0)

    def scores(q_stacked, keys_at):
        keys = jnp.concatenate([k_ref[keys_at, :], bias_ref[head, keys_at, :]], axis=1)
        return jnp.dot(keys, q_stacked, preferred_element_type=F32)

    per_query_tile = []
    for qi in range(n_tiles):
        spans = []
        k0 = 0
        while k0 < qi * tq:
            k1 = min(k0 + ATTN_KEY_SPAN, qi * tq)
            spans.append((qi, slice(k0, k1), k0 == 0, False))
            k0 = k1
        spans.append((qi, slice(qi * tq, (qi + 1) * tq), qi == 0, True))
        per_query_tile.append(spans)
    tiles = [spans[r] for r in range(max(map(len, per_query_tile)))
             for spans in per_query_tile if r < len(spans)]

    def queries_for(qi):
        if qi not in stacked_done:
            qs_ref[qi] = stacked_queries(qi)
            stacked_done.add(qi)
        return qs_ref[qi]

    stacked_done = set()
    values_done = set()
    m_of = {}
    acc_of = {}
    s_next = scores(queries_for(tiles[0][0]), tiles[0][1])
    for t, (qi, keys_at, first, diagonal) in enumerate(tiles):
        s = s_next
        if t + 1 < len(tiles):
            s_next = scores(queries_for(tiles[t + 1][0]), tiles[t + 1][1])
        if diagonal:
            key_row = lax.broadcasted_iota(jnp.int32, (tq, 2 * tq), 0)
            query_col = lax.broadcasted_iota(jnp.int32, (tq, 2 * tq), 1) & (tq - 1)
            s = jnp.where(key_row <= query_col, s, MASKED_SCORE)
        for k0 in range(keys_at.start, keys_at.stop, tq):
            if k0 not in values_done:
                vt_ref[0:HEAD_WIDTH, k0:k0 + tq] = v_ref[k0:k0 + tq, :].T
                values_done.add(k0)
        values_t = vt_ref[:, keys_at]
        if first:
            m = jnp.max(s, axis=0, keepdims=True)
            acc = jnp.dot(values_t, jnp.exp2(s - m).astype(BF16),
                          preferred_element_type=F32)
        else:
            m = jnp.maximum(m_of[qi], jnp.max(s, axis=0, keepdims=True))
            acc = jnp.exp2(m_of[qi] - m) * acc_of[qi] + jnp.dot(
                values_t, jnp.exp2(s - m).astype(BF16), preferred_element_type=F32)
        m_of[qi], acc_of[qi] = m, acc
        if not diagonal:
            continue

        num = acc[:HEAD_WIDTH]
        inv_den = 1.0 / acc[HEAD_WIDTH:HEAD_WIDTH + 1]
        o_t = num[:, :tq] * inv_den[:, :tq] - num[:, tq:] * (lam * inv_den[:, tq:])
        o_t = o_t * lax.rsqrt(jnp.mean(o_t * o_t, axis=0, keepdims=True) + NORM_EPS)
        g = g_ref[qi * tq:(qi + 1) * tq, :].astype(F32)
        silu_g = g * _sigmoid_of_scaled(g * (-LOG2E))
        o_ref[qi * tq:(qi + 1) * tq, :] = ((o_t.T * out_scale) * silu_g).astype(o_ref.dtype)


def _diff_attention(proj, lq1, lk1, lq2, lk2, subln_w, lam_init):
    batch, seq, n_proj = proj.shape
    width = n_proj // 4
    n_heads = width // HEAD_WIDTH
    assert seq % ATTN_Q_TILE == 0
    d = lq1.shape[-1]

    def head_block(section):
        return pl.BlockSpec((None, seq, HEAD_WIDTH),
                            lambda b, h, s=section: (b, 0, s * n_heads + h))

    def small(n):
        return pl.BlockSpec((1, n), lambda b, h: (0, 0))

    blocks = 5 * seq * HEAD_WIDTH * 2
    temps = 3 * seq * HEAD_WIDTH * 2 + 32 * (2 * ATTN_Q_TILE) * ATTN_Q_TILE * 4
    return pl.pallas_call(
        functools.partial(_diff_attn_kernel, lam_init=lam_init, n_heads=n_heads),
        grid=(batch, n_heads),
        in_specs=[small(d), small(d), small(d), small(d), small(HEAD_WIDTH),
                  head_block(0), head_block(1), head_block(2), head_block(3)],
        out_specs=pl.BlockSpec((None, seq, HEAD_WIDTH), lambda b, h: (b, 0, h)),
        out_shape=jax.ShapeDtypeStruct((batch, seq, width), BF16),
        scratch_shapes=[
            pltpu.VMEM((n_heads, seq, HEAD_WIDTH), BF16),
            pltpu.VMEM((HEAD_WIDTH + 2 * V7X_SUBLANES, seq), BF16),
            pltpu.VMEM((seq // ATTN_Q_TILE, 2 * HEAD_WIDTH, 2 * ATTN_Q_TILE), BF16),
        ],
        compiler_params=pltpu.CompilerParams(
            dimension_semantics=("arbitrary", "arbitrary"),
            vmem_limit_bytes=_vmem_limit(blocks, temps)),
        name="diff_attn",
    )(lq1.reshape(1, d), lk1.reshape(1, d), lq2.reshape(1, d), lk2.reshape(1, d),
      subln_w.reshape(1, HEAD_WIDTH), proj, proj, proj, proj)


def _sigmoid_of_scaled(x_scaled):
    return 1.0 / (1.0 + jnp.exp2(x_scaled))


def _one_minus_square(log2_a):
    t = jnp.tanh(log2_a * (1.0 / LOG2E))
    return (-2.0 * t) / (1.0 - t)


def _sqrt_nonneg(x):
    positive = x > 0.0
    return jnp.where(positive, x * lax.rsqrt(jnp.where(positive, x, 1.0)), 0.0)


def _rglru_proj_kernel(xr_ref, g_ref, cw_ref, cb_ref, wa_ref, ba_ref, wx_ref, bx_ref, la_ref,
                       wo_ref, pw_ref, x_ref, *rest, tiles_per_row, has_next, scaled_cols,
                       col_scale):
    if has_next:
        nw_ref, wi_ref, xo_ref, po_ref = rest[:4]
    else:
        xo_ref = rest[0]
    halo_ref, hc_ref, xc_ref, za_ref, zx_ref, h_ref, y_ref = rest[-7:]
    tt, c = xr_ref.shape
    sub = V7X_SUBLANES
    blk = c // LRU_BLOCKS
    step = pl.program_id(0)

    @pl.when(step == 0)
    def _():
        y_ref[...] = jnp.zeros_like(y_ref)

    @pl.when(lax.rem(step, tiles_per_row) == 0)
    def _():
        halo_ref[...] = jnp.zeros_like(halo_ref)
        hc_ref[...] = jnp.zeros_like(hc_ref)

    row = lax.broadcasted_iota(jnp.int32, (sub, c), 0)

    x_all = xr_ref[...].astype(F32)
    taps = [cw_ref[j:j + 1, :] for j in range(CONV_WIDTH)]
    conv_b = cb_ref[...]
    delays = range(1, CONV_WIDTH)
    prev_rolled = [pltpu.roll(halo_ref[...], d, 0) for d in delays]
    for g0 in range(0, tt, sub):
        x = x_all[g0:g0 + sub]
        rolled = [pltpu.roll(x, d, 0) for d in delays]
        xc = conv_b + taps[CONV_WIDTH - 1] * x
        for d in delays:
            delayed = jnp.where(row >= d, rolled[d - 1], prev_rolled[d - 1])
            xc = xc + taps[CONV_WIDTH - 1 - d] * delayed
        xc_ref[g0:g0 + sub, :] = xc
        prev_rolled = rolled
    halo_ref[...] = x_all[tt - sub:tt]

    xc_bf = xc_ref[...].astype(BF16)
    for blk_i in range(LRU_BLOCKS):
        cols = slice(blk_i * blk, (blk_i + 1) * blk)
        za_ref[:, cols] = jnp.dot(xc_bf[:, cols], wa_ref[blk_i], preferred_element_type=F32)
        zx_ref[:, cols] = jnp.dot(xc_bf[:, cols], wx_ref[blk_i], preferred_element_type=F32)

    out = jnp.dot(y_ref[...], wo_ref[...], preferred_element_type=F32)
    x_new = x_ref[...] + (out * _rms_scale(out)) * pw_ref[...]
    xo_ref[...] = x_new
    if has_next:
        _norm_proj_body(x_new, nw_ref, wi_ref, po_ref, scaled_cols, col_scale)

    z = -la_ref[...]
    softplus = jnp.maximum(z, 0.0) + jnp.log1p(jnp.exp(-jnp.abs(z)))
    log2_a_scale = (-LRU_C * LOG2E) * softplus
    gate_a_b = ba_ref[...] * (-LOG2E)
    gate_x_b = bx_ref[...] * (-LOG2E)

    shifts = [s for s in (1, 2, 4) if s < sub]
    valid = [row >= s for s in shifts]
    h_last = hc_ref[...]
    for g0 in range(0, tt, sub):
        rows = slice(g0, g0 + sub)
        xc = xc_ref[rows, :]
        r = _sigmoid_of_scaled(za_ref[rows, :] + gate_a_b)
        i = _sigmoid_of_scaled(zx_ref[rows, :] + gate_x_b)
        log2_a = log2_a_scale * r
        a = jnp.exp2(log2_a)
        b = _sqrt_nonneg(_one_minus_square(log2_a)) * (i * xc)
        for s, ok in zip(shifts, valid):
            a_prev = jnp.where(ok, pltpu.roll(a, s, 0), 1.0)
            b_prev = jnp.where(ok, pltpu.roll(b, s, 0), 0.0)
            b = a * b_prev + b
            a = a * a_prev
        h = a * h_last + b
        h_ref[rows, :] = h
        h_last = h[sub - 1:sub]
    hc_ref[...] = h_last

    g = g_ref[...].astype(F32)
    y_ref[...] =(h_ref[...] * (g * _sigmoid_of_scaled(g * (-LOG2E)))).astype(y_ref.dtype)


def _rglru_proj(proj, conv_w, conv_b, wa_bf16, ba, wx_bf16, bx, log_a_param, w_out_bf16, post_w,
                x2d, norm_w=None, w_in_bf16=None, *, scaled_cols=0, col_scale=1.0):
    batch, seq, n_proj = proj.shape
    c = n_proj // 2
    tokens, d = x2d.shape
    tt = LRU_TIME_TILE
    blk = c // LRU_BLOCKS
    assert seq % tt == 0 and tt % V7X_SUBLANES == 0
    tiles_per_row = seq // tt
    n_tiles = batch * tiles_per_row
    has_next = w_in_bf16 is not None

    def lru_tile(col):
        def index(s):
            tile = jnp.minimum(s, n_tiles - 1)
            return tile // tiles_per_row, tile % tiles_per_row, col
        return pl.BlockSpec((None, tt, c), index)

    def previous_tile(width):
        return pl.BlockSpec((tt, width), lambda s: (jnp.maximum(s - 1, 0), 0))

    def whole(shape):
        return pl.BlockSpec(shape, lambda s: (0,) * len(shape))

    in_specs = [lru_tile(0), lru_tile(1), whole((CONV_WIDTH, c)), whole((1, c)),
                whole((LRU_BLOCKS, blk, blk)), whole((1, c)), whole((LRU_BLOCKS, blk, blk)),
                whole((1, c)), whole((1, c)), whole((c, d)), whole((1, d)), previous_tile(d)]
    operands = [proj, proj, conv_w, conv_b.reshape(1, c), wa_bf16, ba.reshape(1, c), wx_bf16,
                bx.reshape(1, c), log_a_param.reshape(1, c), w_out_bf16, post_w.reshape(1, d),
                x2d]
    out_specs = [previous_tile(d)]
    out_shape = [jax.ShapeDtypeStruct((tokens, d), F32)]
    blocks = (2 * tt * c * 2 + 2 * LRU_BLOCKS * blk * blk * 2 + c * d * 2 + 2 * tt * d * 4
              + 8 * c * 4)
    if has_next:
        n_out = w_in_bf16.shape[1]
        assert n_out % PROJ_N_CHUNK == 0 and scaled_cols % PROJ_N_CHUNK == 0
        in_specs += [whole((1, d)), whole((d, n_out))]
        operands += [norm_w.reshape(1, d), w_in_bf16]
        out_specs.append(previous_tile(n_out))
        out_shape.append(jax.ShapeDtypeStruct((tokens, n_out), BF16))
        blocks += d * n_out * 2 + tt * n_out * 2
    scratch_bytes = 4 * tt * c * 4 + tt * c * 2
    temps = scratch_bytes + 6 * tt * d * 4
    return pl.pallas_call(
        functools.partial(_rglru_proj_kernel, tiles_per_row=tiles_per_row, has_next=has_next,
                          scaled_cols=scaled_cols, col_scale=col_scale),
        grid=(n_tiles + 1,),
        in_specs=in_specs,
        out_specs=out_specs,
        out_shape=out_shape,
        scratch_shapes=[
            pltpu.VMEM((V7X_SUBLANES, c), F32),
            pltpu.VMEM((1, c), F32),
            pltpu.VMEM((tt, c), F32),
            pltpu.VMEM((tt, c), F32),
            pltpu.VMEM((tt, c), F32),
            pltpu.VMEM((tt, c), F32),
            pltpu.VMEM((tt, c), BF16),
        ],
        compiler_params=pltpu.CompilerParams(
            dimension_semantics=("arbitrary",),
            vmem_limit_bytes=_vmem_limit(blocks, temps)),
        name="rglru_proj",
    )(*operands)


def kernel(x, pre_norm_w, post_norm_w, attn_w_in, attn_w_out, attn_lambda_q1, attn_lambda_k1,
           attn_lambda_q2, attn_lambda_k2, attn_subln_w, lru_w_in, lru_conv_w, lru_conv_b,
           lru_gate_a_w, lru_gate_a_b, lru_gate_x_w, lru_gate_x_b, lru_log_a_param, lru_w_out):
    batch, seq, d = x.shape
    depth = pre_norm_w.shape[0]
    assert depth % N_MIXERS == 0, "the last layer must be an RG-LRU layer"
    x2d = x.reshape(batch * seq, d)

    def in_proj_args(layer):
        j = layer // N_MIXERS
        if layer % N_MIXERS == 0:
            return attn_w_in[j].astype(BF16), dict(
                scaled_cols=attn_w_out.shape[1], col_scale=ATTN_HEAD_DIM ** -0.5 * LOG2E)
        return lru_w_in[j].astype(BF16), {}

    w_in, in_kwargs = in_proj_args(0)
    proj = _norm_proj(x2d, pre_norm_w[0], w_in, **in_kwargs)
    for layer in range(depth):
        j = layer // N_MIXERS
        if layer + 1 < depth:
            w_in, in_kwargs = in_proj_args(layer + 1)
            next_args = (pre_norm_w[layer + 1], w_in)
        else:
            next_args, in_kwargs = (), {}
        if layer % N_MIXERS == 0:
            width = attn_w_out.shape[1]
            lam_init = 0.8 - 0.6 * math.exp(-0.3 * layer)
            y = _diff_attention(proj.reshape(batch, seq, 4 * width), attn_lambda_q1[j],
                                attn_lambda_k1[j], attn_lambda_q2[j], attn_lambda_k2[j],
                                attn_subln_w[j], lam_init)
            outs = _out_in_proj(y.reshape(batch * seq, width), attn_w_out[j].astype(BF16),
                                post_norm_w[layer], x2d, *next_args, **in_kwargs)
        else:
            width = lru_w_out.shape[1]
            outs = _rglru_proj(proj.reshape(batch, seq, 2 * width), lru_conv_w[j], lru_conv_b[j],
                               (lru_gate_a_w[j] * (-LOG2E)).astype(BF16), lru_gate_a_b[j],
                               (lru_gate_x_w[j] * (-LOG2E)).astype(BF16), lru_gate_x_b[j],
                               lru_log_a_param[j], lru_w_out[j].astype(BF16),
                               post_norm_w[layer], x2d, *next_args, **in_kwargs)
        x2d = outs[0]
        if len(outs) > 1:
            proj = outs[1]
    return x2d.reshape(batch, seq, d)
```

```python
import functools
import math

import jax
import jax.numpy as jnp
from jax import lax
from jax.experimental import pallas as pl
from jax.experimental.pallas import tpu as pltpu

F32 = jnp.float32
BF16 = jnp.bfloat16

N_MIXERS = 2
ATTN_HEADS = 8
ATTN_HEAD_DIM = 64
HEAD_WIDTH = 2 * ATTN_HEAD_DIM
LRU_BLOCKS = 4
CONV_WIDTH = 4
LRU_C = 8.0
NORM_EPS = 1e-6
LOG2E = math.log2(math.e)
MASKED_SCORE = -1e30

V7X_VMEM_BYTES = 64 * 1024 * 1024
V7X_SUBLANES = 8
V7X_LANES = 128

TOKEN_TILE = 1024
PROJ_N_CHUNK = 512
ATTN_Q_TILE = 256
ATTN_KEY_SPAN = 512
LRU_TIME_TILE = 512


def _vmem_limit(block_bytes, temp_bytes):
    return int(min(V7X_VMEM_BYTES - (4 << 20), 2 * block_bytes + temp_bytes + (4 << 20)))


def _rms_scale(x):
    return lax.rsqrt(jnp.mean(x * x, axis=-1, keepdims=True) + NORM_EPS)


def _norm_proj_body(x, nw_ref, w_ref, o_ref, scaled_cols, col_scale):
    xn = ((x * _rms_scale(x)) * nw_ref[...]).astype(BF16)
    n_out = o_ref.shape[-1]
    for n0 in range(0, n_out, PROJ_N_CHUNK):
        acc = jnp.dot(xn, w_ref[:, n0:n0 + PROJ_N_CHUNK], preferred_element_type=F32)
        if n0 < scaled_cols:
            acc = acc * col_scale
        o_ref[:, n0:n0 + PROJ_N_CHUNK] = acc.astype(o_ref.dtype)


def _norm_proj_kernel(x_ref, nw_ref, w_ref, o_ref, *, scaled_cols, col_scale):
    _norm_proj_body(x_ref[...], nw_ref, w_ref, o_ref, scaled_cols, col_scale)


def _norm_proj(x2d, norm_w, w_bf16, *, scaled_cols=0, col_scale=1.0):
    tokens, d = x2d.shape
    n_out = w_bf16.shape[1]
    assert tokens % TOKEN_TILE == 0 and n_out % PROJ_N_CHUNK == 0
    assert scaled_cols % PROJ_N_CHUNK == 0
    blocks = TOKEN_TILE * d * 4 + d * 4 + d * n_out * 2 + TOKEN_TILE * n_out * 2
    temps = TOKEN_TILE * d * 6 + 2 * TOKEN_TILE * PROJ_N_CHUNK * 4
    return pl.pallas_call(
        functools.partial(_norm_proj_kernel, scaled_cols=scaled_cols, col_scale=col_scale),
        grid=(tokens // TOKEN_TILE,),
        in_specs=[
            pl.BlockSpec((TOKEN_TILE, d), lambda i: (i, 0)),
            pl.BlockSpec((1, d), lambda i: (0, 0)),
            pl.BlockSpec((d, n_out), lambda i: (0, 0)),
        ],
        out_specs=pl.BlockSpec((TOKEN_TILE, n_out), lambda i: (i, 0)),
        out_shape=jax.ShapeDtypeStruct((tokens, n_out), BF16),
        compiler_params=pltpu.CompilerParams(
            dimension_semantics=("arbitrary",),
            vmem_limit_bytes=_vmem_limit(blocks, temps)),
        name="norm_proj",
    )(x2d, norm_w.reshape(1, d), w_bf16)


def _out_in_proj_kernel(y_ref, wo_ref, pw_ref, x_ref, nw_ref, wi_ref, xo_ref, po_ref, *,
                        scaled_cols, col_scale):
    out = jnp.dot(y_ref[...], wo_ref[...], preferred_element_type=F32)
    x_new = x_ref[...] + (out * _rms_scale(out)) * pw_ref[...]
    xo_ref[...] = x_new
    _norm_proj_body(x_new, nw_ref, wi_ref, po_ref, scaled_cols, col_scale)


def _out_in_proj(y2d, w_out_bf16, post_w, x2d, norm_w, w_in_bf16, *, scaled_cols=0,
                 col_scale=1.0):
    tokens, d = x2d.shape
    width = y2d.shape[1]
    n_out = w_in_bf16.shape[1]
    assert tokens % TOKEN_TILE == 0 and n_out % PROJ_N_CHUNK == 0
    assert scaled_cols % PROJ_N_CHUNK == 0
    blocks = (TOKEN_TILE * width * 2 + width * d * 2 + 2 * d * 4 + 2 * TOKEN_TILE * d * 4
              + d * n_out * 2 + TOKEN_TILE * n_out * 2)
    temps = TOKEN_TILE * d * 10 + 2 * TOKEN_TILE * PROJ_N_CHUNK * 4
    row = pl.BlockSpec((1, d), lambda i: (0, 0))
    return pl.pallas_call(
        functools.partial(_out_in_proj_kernel, scaled_cols=scaled_cols, col_scale=col_scale),
        grid=(tokens // TOKEN_TILE,),
        in_specs=[
            pl.BlockSpec((TOKEN_TILE, width), lambda i: (i, 0)),
            pl.BlockSpec((width, d), lambda i: (0, 0)),
            row,
            pl.BlockSpec((TOKEN_TILE, d), lambda i: (i, 0)),
            row,
            pl.BlockSpec((d, n_out), lambda i: (0, 0)),
        ],
        out_specs=[pl.BlockSpec((TOKEN_TILE, d), lambda i: (i, 0)),
                   pl.BlockSpec((TOKEN_TILE, n_out), lambda i: (i, 0))],
        out_shape=[jax.ShapeDtypeStruct((tokens, d), F32),
                   jax.ShapeDtypeStruct((tokens, n_out), BF16)],
        compiler_params=pltpu.CompilerParams(
            dimension_semantics=("arbitrary",),
            vmem_limit_bytes=_vmem_limit(blocks, temps)),
        name="out_in_proj",
    )(y2d, w_out_bf16, post_w.reshape(1, d), x2d, norm_w.reshape(1, d), w_in_bf16)


def _diff_attn_kernel(lq1_ref, lk1_ref, lq2_ref, lk2_ref, sw_ref, q_ref, k_ref, v_ref, g_ref,
                      o_ref, bias_ref, vt_ref, qs_ref, *, lam_init, n_heads):
    seq = q_ref.shape[0]
    tq = ATTN_Q_TILE
    n_tiles = seq // tq
    head = pl.program_id(1)

    @pl.when(pl.program_id(0) == 0)
    def _():
        head_p1 = (head + 1).astype(F32)
        slope = jnp.exp2(jnp.full((seq, HEAD_WIDTH), -(8.0 / n_heads), F32) * head_p1) * LOG2E
        key_pos = lax.broadcasted_iota(jnp.int32, (seq, HEAD_WIDTH), 0).astype(F32)
        key_lane = lax.broadcasted_iota(jnp.int32, (seq, HEAD_WIDTH), 1)
        bias = slope * key_pos
        bias_hi = bias.astype(BF16).astype(F32)
        rest = bias - bias_hi
        bias_mid = rest.astype(BF16).astype(F32)
        bias_lo = rest - bias_mid
        pieces = jnp.where(key_lane == 0, bias_hi,
                           jnp.where(key_lane == 1, bias_mid,
                                     jnp.where(key_lane == 2, bias_lo, 0.0)))
        bias_ref[head] = pieces.astype(BF16)

    vt_ref[HEAD_WIDTH:, :] = jnp.ones((vt_ref.shape[0] - HEAD_WIDTH, seq), BF16)
    lam = (jnp.exp(jnp.sum(lq1_ref[...] * lk1_ref[...], keepdims=True))
           - jnp.exp(jnp.sum(lq2_ref[...] * lk2_ref[...], keepdims=True)) + lam_init)

    feature = lax.broadcasted_iota(jnp.int32, (HEAD_WIDTH, tq), 0)
    ones_rows = jnp.where(lax.broadcasted_iota(jnp.int32, (HEAD_WIDTH, 2 * tq), 0) < 3,
                          1.0, 0.0).astype(BF16)
    out_scale = sw_ref[...] * (1.0 - lam_init)

    def stacked_queries(qi):
        q_t = q_ref[qi * tq:(qi + 1) * tq, :].T
        zero = jnp.zeros_like(q_t)
        top = jnp.concatenate([jnp.where(feature < ATTN_HEAD_DIM, q_t, zero),
                               jnp.where(feature >= ATTN_HEAD_DIM, q_t, zero)], axis=1)
        return jnp.concatenate([top, ones_rows], axis=0)

    def scores(q_stacked, keys_at):
        keys = jnp.concatenate([k_ref[keys_at, :], bias_ref[head, keys_at, :]], axis=1)
        return jnp.dot(keys, q_stacked, preferred_element_type=F32)

    per_query_tile = []
    for qi in range(n_tiles):
        spans = []
        k0 = 0
        while k0 < qi * tq:
            k1 = min(k0 + ATTN_KEY_SPAN, qi * tq)
            spans.append((qi, slice(k0, k1), k0 == 0, False))
            k0 = k1
        spans.append((qi, slice(qi * tq, (qi + 1) * tq), qi == 0, True))
        per_query_tile.append(spans)
    tiles = [spans[r] for r in range(max(map(len, per_query_tile)))
             for spans in per_query_tile if r < len(spans)]

    def queries_for(qi):
        if qi not in stacked_done:
            qs_ref[qi] = stacked_queries(qi)
            stacked_done.add(qi)
        return qs_ref[qi]

    stacked_done = set()
    values_done = set()
    m_of = {}
    acc_of = {}
    s_next = scores(queries_for(tiles[0][0]), tiles[0][1])
    for t, (qi, keys_at, first, diagonal) in enumerate(tiles):
        s = s_next
        if t + 1 < len(tiles):
            s_next = scores(queries_for(tiles[t + 1][0]), tiles[t + 1][1])
        if diagonal:
            key_row = lax.broadcasted_iota(jnp.int32, (tq, 2 * tq), 0)
            query_col = lax.broadcasted_iota(jnp.int32, (tq, 2 * tq), 1) & (tq - 1)
            s = jnp.where(key_row <= query_col, s, MASKED_SCORE)
        for k0 in range(keys_at.start, keys_at.stop, tq):
            if k0 not in values_done:
                vt_ref[0:HEAD_WIDTH, k0:k0 + tq] = v_ref[k0:k0 + tq, :].T
                values_done.add(k0)
        values_t = vt_ref[:, keys_at]
        if first:
            m = jnp.max(s, axis=0, keepdims=True)
            acc = jnp.dot(values_t, jnp.exp2(s - m).astype(BF16),
                          preferred_element_type=F32)
        else:
            m = jnp.maximum(m_of[qi], jnp.max(s, axis=0, keepdims=True))
            acc = jnp.exp2(m_of[qi] - m) * acc_of[qi] + jnp.dot(
                values_t, jnp.exp2(s - m).astype(BF16), preferred_element_type=F32)
        m_of[qi], acc_of[qi] = m, acc
        if not diagonal:
            continue

        num = acc[:HEAD_WIDTH]
        inv_den = 1.0 / acc[HEAD_WIDTH:HEAD_WIDTH + 1]
        o_t = num[:, :tq] * inv_den[:, :tq] - num[:, tq:] * (lam * inv_den[:, tq:])
        o_t = o_t * lax.rsqrt(jnp.mean(o_t * o_t, axis=0, keepdims=True) + NORM_EPS)
        g = g_ref[qi * tq:(qi + 1) * tq, :].astype(F32)
        silu_g = g * _sigmoid_of_scaled(g * (-LOG2E))
        o_ref[qi * tq:(qi + 1) * tq, :] = ((o_t.T * out_scale) * silu_g).astype(o_ref.dtype)


def _diff_attention(proj, lq1, lk1, lq2, lk2, subln_w, lam_init):
    batch, seq, n_proj = proj.shape
    width = n_proj // 4
    n_heads = width // HEAD_WIDTH
    assert seq % ATTN_Q_TILE == 0
    d = lq1.shape[-1]

    def head_block(section):
        return pl.BlockSpec((None, seq, HEAD_WIDTH),
                            lambda b, h, s=section: (b, 0, s * n_heads + h))

    def small(n):
        return pl.BlockSpec((1, n), lambda b, h: (0, 0))

    blocks = 5 * seq * HEAD_WIDTH * 2
    temps = 3 * seq * HEAD_WIDTH * 2 + 32 * (2 * ATTN_Q_TILE) * ATTN_Q_TILE * 4
    return pl.pallas_call(
        functools.partial(_diff_attn_kernel, lam_init=lam_init, n_heads=n_heads),
        grid=(batch, n_heads),
        in_specs=[small(d), small(d), small(d), small(d), small(HEAD_WIDTH),
                  head_block(0), head_block(1), head_block(2), head_block(3)],
        out_specs=pl.BlockSpec((None, seq, HEAD_WIDTH), lambda b, h: (b, 0, h)),
        out_shape=jax.ShapeDtypeStruct((batch, seq, width), BF16),
        scratch_shapes=[
            pltpu.VMEM((n_heads, seq, HEAD_WIDTH), BF16),
            pltpu.VMEM((HEAD_WIDTH + 2 * V7X_SUBLANES, seq), BF16),
            pltpu.VMEM((seq // ATTN_Q_TILE, 2 * HEAD_WIDTH, 2 * ATTN_Q_TILE), BF16),
        ],
        compiler_params=pltpu.CompilerParams(
            dimension_semantics=("arbitrary", "arbitrary"),
            vmem_limit_bytes=_vmem_limit(blocks, temps)),
        name="diff_attn",
    )(lq1.reshape(1, d), lk1.reshape(1, d), lq2.reshape(1, d), lk2.reshape(1, d),
      subln_w.reshape(1, HEAD_WIDTH), proj, proj, proj, proj)


def _sigmoid_of_scaled(x_scaled):
    return 1.0 / (1.0 + jnp.exp2(x_scaled))


def _one_minus_square(log2_a):
    t = jnp.tanh(log2_a * (1.0 / LOG2E))
    return (-2.0 * t) / (1.0 - t)


def _sqrt_nonneg(x):
    positive = x > 0.0
    return jnp.where(positive, x * lax.rsqrt(jnp.where(positive, x, 1.0)), 0.0)


def _rglru_proj_kernel(xr_ref, g_ref, cw_ref, cb_ref, wa_ref, ba_ref, wx_ref, bx_ref, la_ref,
                       wo_ref, pw_ref, x_ref, *rest, tiles_per_row, n_tiles, has_next,
                       scaled_cols, col_scale):
    if has_next:
        nw_ref, wi_ref, xo_ref, po_ref = rest[:4]
        xn_ref = rest[-1]
        rest = rest[:-1]
    else:
        xo_ref = rest[0]
    halo_ref, hc_ref, xc_ref, za_ref, zx_ref, h_ref, y_ref = rest[-7:]
    tt, c = xr_ref.shape
    sub = V7X_SUBLANES
    blk = c // LRU_BLOCKS
    step = pl.program_id(0)

    @pl.when(step == 0)
    def _():
        y_ref[...] = jnp.zeros_like(y_ref)
        if has_next:
            xn_ref[...] = jnp.zeros_like(xn_ref)

    @pl.when(lax.rem(step, tiles_per_row) == 0)
    def _():
        halo_ref[...] = jnp.zeros_like(halo_ref)
        hc_ref[...] = jnp.zeros_like(hc_ref)

    row = lax.broadcasted_iota(jnp.int32, (sub, c), 0)

    xn_old = xn_ref[...] if has_next else None

    def in_proj_chunks(starts):
        for n0 in starts:
            acc = jnp.dot(xn_old, wi_ref[:, n0:n0 + PROJ_N_CHUNK],
                          preferred_element_type=F32)
            if n0 < scaled_cols:
                acc = acc * col_scale
            po_ref[:, n0:n0 + PROJ_N_CHUNK] = acc.astype(po_ref.dtype)

    chunk_starts = list(range(0, po_ref.shape[-1], PROJ_N_CHUNK)) if has_next else []
    n_early = len(chunk_starts) // 4
    in_proj_chunks(chunk_starts[:n_early])

    x_all = xr_ref[...].astype(F32)
    taps = [cw_ref[j:j + 1, :] for j in range(CONV_WIDTH)]
    conv_b = cb_ref[...]
    delays = range(1, CONV_WIDTH)
    prev_rolled = [pltpu.roll(halo_ref[...], d, 0) for d in delays]
    for g0 in range(0, tt, sub):
        x = x_all[g0:g0 + sub]
        rolled = [pltpu.roll(x, d, 0) for d in delays]
        xc = conv_b + taps[CONV_WIDTH - 1] * x
        for d in delays:
            delayed = jnp.where(row >= d, rolled[d - 1], prev_rolled[d - 1])
            xc = xc + taps[CONV_WIDTH - 1 - d] * delayed
        xc_ref[g0:g0 + sub, :] = xc
        prev_rolled = rolled
    halo_ref[...] = x_all[tt - sub:tt]

    xc_bf = xc_ref[...].astype(BF16)
    for blk_i in range(LRU_BLOCKS):
        cols = slice(blk_i * blk, (blk_i + 1) * blk)
        za_ref[:, cols] = jnp.dot(xc_bf[:, cols], wa_ref[blk_i], preferred_element_type=F32)
        zx_ref[:, cols] = jnp.dot(xc_bf[:, cols], wx_ref[blk_i], preferred_element_type=F32)

    in_proj_chunks(chunk_starts[n_early:])

    out = jnp.dot(y_ref[...], wo_ref[...], preferred_element_type=F32)
    x_new = x_ref[...] + (out * _rms_scale(out)) * pw_ref[...]
    xo_ref[...] = x_new
    if has_next:
        xn_ref[...] = ((x_new * _rms_scale(x_new)) * nw_ref[...]).astype(BF16)

    z = -la_ref[...]
    softplus = jnp.maximum(z, 0.0) + jnp.log1p(jnp.exp(-jnp.abs(z)))
    log2_a_scale = (-LRU_C * LOG2E) * softplus
    gate_a_b = ba_ref[...] * (-LOG2E)
    gate_x_b = bx_ref[...] * (-LOG2E)

    shifts = [s for s in (1, 2, 4) if s < sub]
    valid = [row >= s for s in shifts]
    h_last = hc_ref[...]
    for g0 in range(0, tt, sub):
        rows = slice(g0, g0 + sub)
        xc = xc_ref[rows, :]
        r = _sigmoid_of_scaled(za_ref[rows, :] + gate_a_b)
        i = _sigmoid_of_scaled(zx_ref[rows, :] + gate_x_b)
        log2_a = log2_a_scale * r
        a = jnp.exp2(log2_a)
        b = _sqrt_nonneg(_one_minus_square(log2_a)) * (i * xc)
        for s, ok in zip(shifts, valid):
            a_prev = jnp.where(ok, pltpu.roll(a, s, 0), 1.0)
            b_prev = jnp.where(ok, pltpu.roll(b, s, 0), 0.0)
            b = a * b_prev + b
            a = a * a_prev
        h = a * h_last + b
        h_ref[rows, :] = h
        h_last = h[sub - 1:sub]
    hc_ref[...] = h_last

    g = g_ref[...].astype(F32)
    y_new = (h_ref[...] * (g * _sigmoid_of_scaled(g * (-LOG2E)))).astype(y_ref.dtype)
    if has_next:
        y_new = jnp.where(step < n_tiles, y_new, y_ref[...])
    y_ref[...] = y_new


def _rglru_proj(proj, conv_w, conv_b, wa_bf16, ba, wx_bf16, bx, log_a_param, w_out_bf16, post_w,
                x2d, norm_w=None, w_in_bf16=None, *, scaled_cols=0, col_scale=1.0):
    batch, seq, n_proj = proj.shape
    c = n_proj // 2
    tokens, d = x2d.shape
    tt = LRU_TIME_TILE
    blk = c // LRU_BLOCKS
    assert seq % tt == 0 and tt % V7X_SUBLANES == 0
    tiles_per_row = seq // tt
    n_tiles = batch * tiles_per_row
    has_next = w_in_bf16 is not None

    def lru_tile(col):
        def index(s):
            tile = jnp.minimum(s, n_tiles - 1)
            return tile // tiles_per_row, tile % tiles_per_row, col
        return pl.BlockSpec((None, tt, c), index)

    def lagging_tile(width, lag):
        return pl.BlockSpec((tt, width), lambda s: (jnp.clip(s - lag, 0, n_tiles - 1), 0))

    def whole(shape):
        return pl.BlockSpec(shape, lambda s: (0,) * len(shape))

    in_specs = [lru_tile(0), lru_tile(1), whole((CONV_WIDTH, c)), whole((1, c)),
                whole((LRU_BLOCKS, blk, blk)), whole((1, c)), whole((LRU_BLOCKS, blk, blk)),
                whole((1, c)), whole((1, c)), whole((c, d)), whole((1, d)), lagging_tile(d, 1)]
    operands = [proj, proj, conv_w, conv_b.reshape(1, c), wa_bf16, ba.reshape(1, c), wx_bf16,
                bx.reshape(1, c), log_a_param.reshape(1, c), w_out_bf16, post_w.reshape(1, d),
                x2d]
    out_specs = [lagging_tile(d, 1)]
    out_shape = [jax.ShapeDtypeStruct((tokens, d), F32)]
    blocks = (2 * tt * c * 2 + 2 * LRU_BLOCKS * blk * blk * 2 + c * d * 2 + 2 * tt * d * 4
              + 8 * c * 4)
    scratch_shapes = [
        pltpu.VMEM((V7X_SUBLANES, c), F32),
        pltpu.VMEM((1, c), F32),
        pltpu.VMEM((tt, c), F32),
        pltpu.VMEM((tt, c), F32),
        pltpu.VMEM((tt, c), F32),
        pltpu.VMEM((tt, c), F32),
        pltpu.VMEM((tt, c), BF16),
    ]
    scratch_bytes = 4 * tt * c * 4 + tt * c * 2
    pipeline_depth = 1
    if has_next:
        n_out = w_in_bf16.shape[1]
        assert n_out % PROJ_N_CHUNK == 0 and scaled_cols % PROJ_N_CHUNK == 0
        in_specs += [whole((1, d)), whole((d, n_out))]
        operands += [norm_w.reshape(1, d), w_in_bf16]
        out_specs.append(lagging_tile(n_out, 2))
        out_shape.append(jax.ShapeDtypeStruct((tokens, n_out), BF16))
        blocks += d * n_out * 2 + tt * n_out * 2
        scratch_shapes.append(pltpu.VMEM((tt, d), BF16))
        scratch_bytes += tt * d * 2
        pipeline_depth = 2
    temps = scratch_bytes + 6 * tt * d * 4
    return pl.pallas_call(
        functools.partial(_rglru_proj_kernel, tiles_per_row=tiles_per_row, n_tiles=n_tiles,
                          has_next=has_next, scaled_cols=scaled_cols, col_scale=col_scale),
        grid=(n_tiles + pipeline_depth,),
        in_specs=in_specs,
        out_specs=out_specs,
        out_shape=out_shape,
        scratch_shapes=scratch_shapes,
        compiler_params=pltpu.CompilerParams(
            dimension_semantics=("arbitrary",),
            vmem_limit_bytes=_vmem_limit(blocks, temps)),
        name="rglru_proj",
    )(*operands)


def kernel(x, pre_norm_w, post_norm_w, attn_w_in, attn_w_out, attn_lambda_q1, attn_lambda_k1,
           attn_lambda_q2, attn_lambda_k2, attn_subln_w, lru_w_in, lru_conv_w, lru_conv_b,
           lru_gate_a_w, lru_gate_a_b, lru_gate_x_w, lru_gate_x_b, lru_log_a_param, lru_w_out):
    batch, seq, d = x.shape
    depth = pre_norm_w.shape[0]
    assert depth % N_MIXERS == 0, "the last layer must be an RG-LRU layer"
    x2d = x.reshape(batch * seq, d)

    def in_proj_args(layer):
        j = layer // N_MIXERS
        if layer % N_MIXERS == 0:
            return attn_w_in[j].astype(BF16), dict(
                scaled_cols=attn_w_out.shape[1], col_scale=ATTN_HEAD_DIM ** -0.5 * LOG2E)
        return lru_w_in[j].astype(BF16), {}

    w_in, in_kwargs = in_proj_args(0)
    proj = _norm_proj(x2d, pre_norm_w[0], w_in, **in_kwargs)
    for layer in range(depth):
        j = layer // N_MIXERS
        if layer + 1 < depth:
            w_in, in_kwargs = in_proj_args(layer + 1)
            next_args = (pre_norm_w[layer + 1], w_in)
        else:
            next_args, in_kwargs = (), {}
        if layer % N_MIXERS == 0:
            width = attn_w_out.shape[1]
            lam_init = 0.8 - 0.6 * math.exp(-0.3 * layer)
            y = _diff_attention(proj.reshape(batch, seq, 4 * width), attn_lambda_q1[j],
                                attn_lambda_k1[j], attn_lambda_q2[j], attn_lambda_k2[j],
                                attn_subln_w[j], lam_init)
            outs = _out_in_proj(y.reshape(batch * seq, width), attn_w_out[j].astype(BF16),
                                post_norm_w[layer], x2d, *next_args, **in_kwargs)
        else:
            width = lru_w_out.shape[1]
            outs = _rglru_proj(proj.reshape(batch, seq, 2 * width), lru_conv_w[j], lru_conv_b[j],
                               (lru_gate_a_w[j] * (-LOG2E)).astype(BF16), lru_gate_a_b[j],
                               (lru_gate_x_w[j] * (-LOG2E)).astype(BF16), lru_gate_x_b[j],
                               lru_log_a_param[j], lru_w_out[j].astype(BF16),
                               post_norm_w[layer], x2d, *next_args, **in_kwargs)
        x2d = outs[0]
        if len(outs) > 1:
            proj = outs[1]
    return x2d.reshape(batch, seq, d)
```

```python
import functools
import math

import jax
import jax.numpy as jnp
from jax import lax
from jax.experimental import pallas as pl
from jax.experimental.pallas import tpu as pltpu

F32 = jnp.float32
BF16 = jnp.bfloat16

N_MIXERS = 2
ATTN_HEADS = 8
ATTN_HEAD_DIM = 64
HEAD_WIDTH = 2 * ATTN_HEAD_DIM
LRU_BLOCKS = 4
CONV_WIDTH = 4
LRU_C = 8.0
NORM_EPS = 1e-6
LOG2E = math.log2(math.e)
MASKED_SCORE = -1e30

V7X_VMEM_BYTES = 64 * 1024 * 1024
V7X_SUBLANES = 8
V7X_LANES = 128

TOKEN_TILE = 1024
PROJ_N_CHUNK = 512
ATTN_Q_TILE = 256
ATTN_KEY_SPAN = 512
ATTN_HEADS_PER_STEP = 2
ALIBI_PIECES = 3
LRU_TIME_TILE = 512


def _vmem_limit(block_bytes, temp_bytes):
    return int(min(V7X_VMEM_BYTES - (4 << 20), 2 * block_bytes + temp_bytes + (4 << 20)))


def _rms_scale(x):
    return lax.rsqrt(jnp.mean(x * x, axis=-1, keepdims=True) + NORM_EPS)


def _norm_proj_body(x, nw_ref, w_ref, o_ref, scaled_cols, col_scale):
    xn = ((x * _rms_scale(x)) * nw_ref[...]).astype(BF16)
    n_out = o_ref.shape[-1]
    for n0 in range(0, n_out, PROJ_N_CHUNK):
        acc = jnp.dot(xn, w_ref[:, n0:n0 + PROJ_N_CHUNK], preferred_element_type=F32)
        if n0 < scaled_cols:
            acc = acc * col_scale
        o_ref[:, n0:n0 + PROJ_N_CHUNK] = acc.astype(o_ref.dtype)


def _norm_proj_kernel(x_ref, nw_ref, w_ref, o_ref, *, scaled_cols, col_scale):
    _norm_proj_body(x_ref[...], nw_ref, w_ref, o_ref, scaled_cols, col_scale)


def _norm_proj(x2d, norm_w, w_bf16, *, scaled_cols=0, col_scale=1.0):
    tokens, d = x2d.shape
    n_out = w_bf16.shape[1]
    assert tokens % TOKEN_TILE == 0 and n_out % PROJ_N_CHUNK == 0
    assert scaled_cols % PROJ_N_CHUNK == 0
    blocks = TOKEN_TILE * d * 4 + d * 4 + d * n_out * 2 + TOKEN_TILE * n_out * 2
    temps = TOKEN_TILE * d * 6 + 2 * TOKEN_TILE * PROJ_N_CHUNK * 4
    return pl.pallas_call(
        functools.partial(_norm_proj_kernel, scaled_cols=scaled_cols, col_scale=col_scale),
        grid=(tokens // TOKEN_TILE,),
        in_specs=[
            pl.BlockSpec((TOKEN_TILE, d), lambda i: (i, 0)),
            pl.BlockSpec((1, d), lambda i: (0, 0)),
            pl.BlockSpec((d, n_out), lambda i: (0, 0)),
        ],
        out_specs=pl.BlockSpec((TOKEN_TILE, n_out), lambda i: (i, 0)),
        out_shape=jax.ShapeDtypeStruct((tokens, n_out), BF16),
        compiler_params=pltpu.CompilerParams(
            dimension_semantics=("arbitrary",),
            vmem_limit_bytes=_vmem_limit(blocks, temps)),
        name="norm_proj",
    )(x2d, norm_w.reshape(1, d), w_bf16)


def _out_in_proj_kernel(y_ref, wo_ref, pw_ref, x_ref, nw_ref, wi_ref, xo_ref, po_ref, *,
                        scaled_cols, col_scale):
    out = jnp.dot(y_ref[...], wo_ref[...], preferred_element_type=F32)
    x_new = x_ref[...] + (out * _rms_scale(out)) * pw_ref[...]
    xo_ref[...] = x_new
    _norm_proj_body(x_new, nw_ref, wi_ref, po_ref, scaled_cols, col_scale)


def _out_in_proj(y2d, w_out_bf16, post_w, x2d, norm_w, w_in_bf16, *, scaled_cols=0,
                 col_scale=1.0):
    tokens, d = x2d.shape
    width = y2d.shape[1]
    n_out = w_in_bf16.shape[1]
    assert tokens % TOKEN_TILE == 0 and n_out % PROJ_N_CHUNK == 0
    assert scaled_cols % PROJ_N_CHUNK == 0
    blocks = (TOKEN_TILE * width * 2 + width * d * 2 + 2 * d * 4 + 2 * TOKEN_TILE * d * 4
              + d * n_out * 2 + TOKEN_TILE * n_out * 2)
    temps = TOKEN_TILE * d * 10 + 2 * TOKEN_TILE * PROJ_N_CHUNK * 4
    row = pl.BlockSpec((1, d), lambda i: (0, 0))
    return pl.pallas_call(
        functools.partial(_out_in_proj_kernel, scaled_cols=scaled_cols, col_scale=col_scale),
        grid=(tokens // TOKEN_TILE,),
        in_specs=[
            pl.BlockSpec((TOKEN_TILE, width), lambda i: (i, 0)),
            pl.BlockSpec((width, d), lambda i: (0, 0)),
            row,
            pl.BlockSpec((TOKEN_TILE, d), lambda i: (i, 0)),
            row,
            pl.BlockSpec((d, n_out), lambda i: (0, 0)),
        ],
        out_specs=[pl.BlockSpec((TOKEN_TILE, d), lambda i: (i, 0)),
                   pl.BlockSpec((TOKEN_TILE, n_out), lambda i: (i, 0))],
        out_shape=[jax.ShapeDtypeStruct((tokens, d), F32),
                   jax.ShapeDtypeStruct((tokens, n_out), BF16)],
        compiler_params=pltpu.CompilerParams(
            dimension_semantics=("arbitrary",),
            vmem_limit_bytes=_vmem_limit(blocks, temps)),
        name="out_in_proj",
    )(y2d, w_out_bf16, post_w.reshape(1, d), x2d, norm_w.reshape(1, d), w_in_bf16)


def _diff_attn_kernel(lq1_ref, lk1_ref, lq2_ref, lk2_ref, sw_ref, q_ref, k_ref, v_ref, g_ref,
                      o_ref, bias_ref, vt_ref, qs_ref, *, lam_init, n_heads):
    seq = q_ref.shape[0]
    tq = ATTN_Q_TILE
    n_tiles = seq // tq
    heads_here = q_ref.shape[1] // HEAD_WIDTH
    first_head = pl.program_id(1) * heads_here

    def head_lanes(hh):
        return slice(hh * HEAD_WIDTH, (hh + 1) * HEAD_WIDTH)

    @pl.when(pl.program_id(0) == 0)
    def _():
        for hh in range(heads_here):
            head_p1 = (first_head + hh + 1).astype(F32)
            slope = jnp.exp2(jnp.full((seq, HEAD_WIDTH), -(8.0 / n_heads), F32) * head_p1) * LOG2E
            key_pos = lax.broadcasted_iota(jnp.int32, (seq, HEAD_WIDTH), 0).astype(F32)
            key_lane = lax.broadcasted_iota(jnp.int32, (seq, HEAD_WIDTH), 1)
            bias = slope * key_pos
            bias_hi = bias.astype(BF16).astype(F32)
            rest = bias - bias_hi
            bias_mid = rest.astype(BF16).astype(F32)
            bias_lo = rest - bias_mid
            pieces = jnp.where(key_lane == 0, bias_hi,
                               jnp.where(key_lane == 1, bias_mid,
                                         jnp.where(key_lane == 2, bias_lo, 0.0)))
            bias_ref[first_head + hh] = pieces.astype(BF16)

    for hh in range(heads_here):
        vt_ref[hh, HEAD_WIDTH:, :] = jnp.ones((vt_ref.shape[1] - HEAD_WIDTH, seq), BF16)
    lam = (jnp.exp(jnp.sum(lq1_ref[...] * lk1_ref[...], keepdims=True))
           - jnp.exp(jnp.sum(lq2_ref[...] * lk2_ref[...], keepdims=True)) + lam_init)

    feature = lax.broadcasted_iota(jnp.int32, (HEAD_WIDTH, tq), 0)
    ones_rows = jnp.where(
        lax.broadcasted_iota(jnp.int32, (HEAD_WIDTH, 2 * tq), 0) < ALIBI_PIECES,
        1.0, 0.0).astype(BF16)
    out_scale = sw_ref[...] * (1.0 - lam_init)

    def stacked_queries(hh, qi):
        q_t = q_ref[qi * tq:(qi + 1) * tq, head_lanes(hh)].T
        zero = jnp.zeros_like(q_t)
        top = jnp.concatenate([jnp.where(feature < ATTN_HEAD_DIM, q_t, zero),
                               jnp.where(feature >= ATTN_HEAD_DIM, q_t, zero)], axis=1)
        return jnp.concatenate([top, ones_rows], axis=0)

    def scores(hh, q_stacked, keys_at):
        keys = jnp.concatenate([k_ref[keys_at, head_lanes(hh)],
                                bias_ref[first_head + hh, keys_at, :]], axis=1)
        return jnp.dot(keys, q_stacked, preferred_element_type=F32)

    chains = []
    for hh in range(heads_here):
        for qi in range(n_tiles):
            spans = []
            k0 = 0
            while k0 < qi * tq:
                k1 = min(k0 + ATTN_KEY_SPAN, qi * tq)
                spans.append((hh, qi, slice(k0, k1), k0 == 0, False))
                k0 = k1
            spans.append((hh, qi, slice(qi * tq, (qi + 1) * tq), qi == 0, True))
            chains.append(spans)
    tiles = [spans[r] for r in range(max(map(len, chains)))
             for spans in chains if r < len(spans)]

    def queries_for(hh, qi):
        if (hh, qi) not in stacked_done:
            qs_ref[hh, qi] = stacked_queries(hh, qi)
            stacked_done.add((hh, qi))
        return qs_ref[hh, qi]

    stacked_done = set()
    values_done = set()
    m_of = {}
    acc_of = {}
    s_next = scores(tiles[0][0], queries_for(tiles[0][0], tiles[0][1]), tiles[0][2])
    for t, (hh, qi, keys_at, first, diagonal) in enumerate(tiles):
        s = s_next
        if t + 1 < len(tiles):
            hh_n, qi_n, keys_n = tiles[t + 1][:3]
            s_next = scores(hh_n, queries_for(hh_n, qi_n), keys_n)
        if diagonal:
            key_row = lax.broadcasted_iota(jnp.int32, (tq, 2 * tq), 0)
            query_col = lax.broadcasted_iota(jnp.int32, (tq, 2 * tq), 1) & (tq - 1)
            s = jnp.where(key_row <= query_col, s, MASKED_SCORE)
        for k0 in range(keys_at.start, keys_at.stop, tq):
            if (hh, k0) not in values_done:
                vt_ref[hh, 0:HEAD_WIDTH, k0:k0 + tq] = v_ref[k0:k0 + tq, head_lanes(hh)].T
                values_done.add((hh, k0))
        values_t = vt_ref[hh, :, keys_at]
        chain = (hh, qi)
        if first:
            m = jnp.max(s, axis=0, keepdims=True)
            acc = jnp.dot(values_t, jnp.exp2(s - m).astype(BF16),
                          preferred_element_type=F32)
        else:
            m = jnp.maximum(m_of[chain], jnp.max(s, axis=0, keepdims=True))
            acc = jnp.exp2(m_of[chain] - m) * acc_of[chain] + jnp.dot(
                values_t, jnp.exp2(s - m).astype(BF16), preferred_element_type=F32)
        m_of[chain], acc_of[chain] = m, acc
        if not diagonal:
            continue

        num = acc[:HEAD_WIDTH]
        inv_den = 1.0 / acc[HEAD_WIDTH:HEAD_WIDTH + 1]
        o_t = num[:, :tq] * inv_den[:, :tq] - num[:, tq:] * (lam * inv_den[:, tq:])
        o_t = o_t * lax.rsqrt(jnp.mean(o_t * o_t, axis=0, keepdims=True) + NORM_EPS)
        g = g_ref[qi * tq:(qi + 1) * tq, head_lanes(hh)].astype(F32)
        silu_g = g * _sigmoid_of_scaled(g * (-LOG2E))
        o_ref[qi * tq:(qi + 1) * tq, head_lanes(hh)] = (
            (o_t.T * out_scale) * silu_g).astype(o_ref.dtype)


def _diff_attention(proj, lq1, lk1, lq2, lk2, subln_w, lam_init):
    batch, seq, n_proj = proj.shape
    width = n_proj // 4
    n_heads = width // HEAD_WIDTH
    assert seq % ATTN_Q_TILE == 0
    d = lq1.shape[-1]

    hps = ATTN_HEADS_PER_STEP
    assert n_heads % hps == 0
    groups = n_heads // hps

    def head_block(section):
        return pl.BlockSpec((None, seq, hps * HEAD_WIDTH),
                            lambda b, h, s=section: (b, 0, s * groups + h))

    def small(n):
        return pl.BlockSpec((1, n), lambda b, h: (0, 0))

    blocks = 5 * seq * hps * HEAD_WIDTH * 2
    scratch_bytes = ((n_heads + hps) * seq * HEAD_WIDTH * 2 + hps * 2 * V7X_SUBLANES * seq * 2
                     + hps * (seq // ATTN_Q_TILE) * 2 * HEAD_WIDTH * 2 * ATTN_Q_TILE * 2)
    temps = scratch_bytes + hps * 24 * (2 * ATTN_Q_TILE) * ATTN_Q_TILE * 4
    return pl.pallas_call(
        functools.partial(_diff_attn_kernel, lam_init=lam_init, n_heads=n_heads),
        grid=(batch, groups),
        in_specs=[small(d), small(d), small(d), small(d), small(HEAD_WIDTH),
                  head_block(0), head_block(1), head_block(2), head_block(3)],
        out_specs=pl.BlockSpec((None, seq, hps * HEAD_WIDTH), lambda b, h: (b, 0, h)),
        out_shape=jax.ShapeDtypeStruct((batch, seq, width), BF16),
        scratch_shapes=[
            pltpu.VMEM((n_heads, seq, HEAD_WIDTH), BF16),
            pltpu.VMEM((hps, HEAD_WIDTH + 2 * V7X_SUBLANES, seq), BF16),
            pltpu.VMEM((hps, seq // ATTN_Q_TILE, 2 * HEAD_WIDTH, 2 * ATTN_Q_TILE), BF16),
        ],
        compiler_params=pltpu.CompilerParams(
            dimension_semantics=("arbitrary", "arbitrary"),
            vmem_limit_bytes=_vmem_limit(blocks, temps)),
        name="diff_attn",
    )(lq1.reshape(1, d), lk1.reshape(1, d), lq2.reshape(1, d), lk2.reshape(1, d),
      subln_w.reshape(1, HEAD_WIDTH), proj, proj, proj, proj)


def _sigmoid_of_scaled(x_scaled):
    return 1.0 / (1.0 + jnp.exp2(x_scaled))


def _one_minus_square(log2_a):
    t = jnp.tanh(log2_a * (1.0 / LOG2E))
    return (-2.0 * t) / (1.0 - t)


def _sqrt_nonneg(x):
    positive = x > 0.0
    return jnp.where(positive, x * lax.rsqrt(jnp.where(positive, x, 1.0)), 0.0)


def _rglru_proj_kernel(xr_ref, g_ref, cw_ref, cb_ref, wa_ref, ba_ref, wx_ref, bx_ref, la_ref,
                       wo_ref, pw_ref, x_ref, *rest, tiles_per_row, n_tiles, has_next,
                       scaled_cols, col_scale):
    if has_next:
        nw_ref, wi_ref, xo_ref, po_ref = rest[:4]
        xn_ref = rest[-1]
        rest = rest[:-1]
    else:
        xo_ref = rest[0]
    halo_ref, hc_ref, xc_ref, za_ref, zx_ref, h_ref, y_ref = rest[-7:]
    tt, c = xr_ref.shape
    sub = V7X_SUBLANES
    blk = c // LRU_BLOCKS
    step = pl.program_id(0)

    @pl.when(step == 0)
    def _():
        y_ref[...] = jnp.zeros_like(y_ref)
        if has_next:
            xn_ref[...] = jnp.zeros_like(xn_ref)

    @pl.when(lax.rem(step, tiles_per_row) == 0)
    def _():
        halo_ref[...] = jnp.zeros_like(halo_ref)
        hc_ref[...] = jnp.zeros_like(hc_ref)

    row = lax.broadcasted_iota(jnp.int32, (sub, c), 0)

    xn_old = xn_ref[...] if has_next else None

    def in_proj_chunks(starts):
        for n0 in starts:
            acc = jnp.dot(xn_old, wi_ref[:, n0:n0 + PROJ_N_CHUNK],
                          preferred_element_type=F32)
            if n0 < scaled_cols:
                acc = acc * col_scale
            po_ref[:, n0:n0 + PROJ_N_CHUNK] = acc.astype(po_ref.dtype)

    chunk_starts = list(range(0, po_ref.shape[-1], PROJ_N_CHUNK)) if has_next else []
    n_early = len(chunk_starts) // 4
    in_proj_chunks(chunk_starts[:n_early])

    x_all = xr_ref[...].astype(F32)
    taps = [cw_ref[j:j + 1, :] for j in range(CONV_WIDTH)]
    conv_b = cb_ref[...]
    delays = range(1, CONV_WIDTH)
    prev_rolled = [pltpu.roll(halo_ref[...], d, 0) for d in delays]
    for g0 in range(0, tt, sub):
        x = x_all[g0:g0 + sub]
        rolled = [pltpu.roll(x, d, 0) for d in delays]
        xc = conv_b + taps[CONV_WIDTH - 1] * x
        for d in delays:
            delayed = jnp.where(row >= d, rolled[d - 1], prev_rolled[d - 1])
            xc = xc + taps[CONV_WIDTH - 1 - d] * delayed
        xc_ref[g0:g0 + sub, :] = xc
        prev_rolled = rolled
    halo_ref[...] = x_all[tt - sub:tt]

    xc_bf = xc_ref[...].astype(BF16)
    for blk_i in range(LRU_BLOCKS):
        cols = slice(blk_i * blk, (blk_i + 1) * blk)
        za_ref[:, cols] = jnp.dot(xc_bf[:, cols], wa_ref[blk_i], preferred_element_type=F32)
        zx_ref[:, cols] = jnp.dot(xc_bf[:, cols], wx_ref[blk_i], preferred_element_type=F32)

    in_proj_chunks(chunk_starts[n_early:])

    out = jnp.dot(y_ref[...], wo_ref[...], preferred_element_type=F32)
    x_new = x_ref[...] + (out * _rms_scale(out)) * pw_ref[...]
    xo_ref[...] = x_new
    if has_next:
        xn_ref[...] = ((x_new * _rms_scale(x_new)) * nw_ref[...]).astype(BF16)

    z = -la_ref[...]
    softplus = jnp.maximum(z, 0.0) + jnp.log1p(jnp.exp(-jnp.abs(z)))
    log2_a_scale = (-LRU_C * LOG2E) * softplus
    gate_a_b = ba_ref[...] * (-LOG2E)
    gate_x_b = bx_ref[...] * (-LOG2E)

    shifts = [s for s in (1, 2, 4) if s < sub]
    valid = [row >= s for s in shifts]
    h_last = hc_ref[...]
    for g0 in range(0, tt, sub):
        rows = slice(g0, g0 + sub)
        xc = xc_ref[rows, :]
        r = _sigmoid_of_scaled(za_ref[rows, :] + gate_a_b)
        i = _sigmoid_of_scaled(zx_ref[rows, :] + gate_x_b)
        log2_a = log2_a_scale * r
        a = jnp.exp2(log2_a)
        b = _sqrt_nonneg(_one_minus_square(log2_a)) * (i * xc)
        for s, ok in zip(shifts, valid):
            a_prev = jnp.where(ok, pltpu.roll(a, s, 0), 1.0)
            b_prev = jnp.where(ok, pltpu.roll(b, s, 0), 0.0)
            b = a * b_prev + b
            a = a * a_prev
        h = a * h_last + b
        h_ref[rows, :] = h
        h_last = h[sub - 1:sub]
    hc_ref[...] = h_last

    g = g_ref[...].astype(F32)
    y_new = (h_ref[...] * (g * _sigmoid_of_scaled(g * (-LOG2E)))).astype(y_ref.dtype)
    if has_next:
        y_new = jnp.where(step < n_tiles, y_new, y_ref[...])
    y_ref[...] = y_new


def _rglru_proj(proj, conv_w, conv_b, wa_bf16, ba, wx_bf16, bx, log_a_param, w_out_bf16, post_w,
                x2d, norm_w=None, w_in_bf16=None, *, scaled_cols=0, col_scale=1.0):
    batch, seq, n_proj = proj.shape
    c = n_proj // 2
    tokens, d = x2d.shape
    tt = LRU_TIME_TILE
    blk = c // LRU_BLOCKS
    assert seq % tt == 0 and tt % V7X_SUBLANES == 0
    tiles_per_row = seq // tt
    n_tiles = batch * tiles_per_row
    has_next = w_in_bf16 is not None

    def lru_tile(col):
        def index(s):
            tile = jnp.minimum(s, n_tiles - 1)
            return tile // tiles_per_row, tile % tiles_per_row, col
        return pl.BlockSpec((None, tt, c), index)

    def lagging_tile(width, lag):
        return pl.BlockSpec((tt, width), lambda s: (jnp.clip(s - lag, 0, n_tiles - 1), 0))

    def whole(shape):
        return pl.BlockSpec(shape, lambda s: (0,) * len(shape))

    in_specs = [lru_tile(0), lru_tile(1), whole((CONV_WIDTH, c)), whole((1, c)),
                whole((LRU_BLOCKS, blk, blk)), whole((1, c)), whole((LRU_BLOCKS, blk, blk)),
                whole((1, c)), whole((1, c)), whole((c, d)), whole((1, d)), lagging_tile(d, 1)]
    operands = [proj, proj, conv_w, conv_b.reshape(1, c), wa_bf16, ba.reshape(1, c), wx_bf16,
                bx.reshape(1, c), log_a_param.reshape(1, c), w_out_bf16, post_w.reshape(1, d),
                x2d]
    out_specs = [lagging_tile(d, 1)]
    out_shape = [jax.ShapeDtypeStruct((tokens, d), F32)]
    blocks = (2 * tt * c * 2 + 2 * LRU_BLOCKS * blk * blk * 2 + c * d * 2 + 2 * tt * d * 4
              + 8 * c * 4)
    scratch_shapes = [
        pltpu.VMEM((V7X_SUBLANES, c), F32),
        pltpu.VMEM((1, c), F32),
        pltpu.VMEM((tt, c), F32),
        pltpu.VMEM((tt, c), F32),
        pltpu.VMEM((tt, c), F32),
        pltpu.VMEM((tt, c), F32),
        pltpu.VMEM((tt, c), BF16),
    ]
    scratch_bytes = 4 * tt * c * 4 + tt * c * 2
    pipeline_depth = 1
    if has_next:
        n_out = w_in_bf16.shape[1]
        assert n_out % PROJ_N_CHUNK == 0 and scaled_cols % PROJ_N_CHUNK == 0
        in_specs += [whole((1, d)), whole((d, n_out))]
        operands += [norm_w.reshape(1, d), w_in_bf16]
        out_specs.append(lagging_tile(n_out, 2))
        out_shape.append(jax.ShapeDtypeStruct((tokens, n_out), BF16))
        blocks += d * n_out * 2 + tt * n_out * 2
        scratch_shapes.append(pltpu.VMEM((tt, d), BF16))
        scratch_bytes += tt * d * 2
        pipeline_depth = 2
    temps = scratch_bytes + 6 * tt * d * 4
    return pl.pallas_call(
        functools.partial(_rglru_proj_kernel, tiles_per_row=tiles_per_row, n_tiles=n_tiles,
                          has_next=has_next, scaled_cols=scaled_cols, col_scale=col_scale),
        grid=(n_tiles + pipeline_depth,),
        in_specs=in_specs,
        out_specs=out_specs,
        out_shape=out_shape,
        scratch_shapes=scratch_shapes,
        compiler_params=pltpu.CompilerParams(
            dimension_semantics=("arbitrary",),
            vmem_limit_bytes=_vmem_limit(blocks, temps)),
        name="rglru_proj",
    )(*operands)


def kernel(x, pre_norm_w, post_norm_w, attn_w_in, attn_w_out, attn_lambda_q1, attn_lambda_k1,
           attn_lambda_q2, attn_lambda_k2, attn_subln_w, lru_w_in, lru_conv_w, lru_conv_b,
           lru_gate_a_w, lru_gate_a_b, lru_gate_x_w, lru_gate_x_b, lru_log_a_param, lru_w_out):
    batch, seq, d = x.shape
    depth = pre_norm_w.shape[0]
    assert depth % N_MIXERS == 0, "the last layer must be an RG-LRU layer"
    x2d = x.reshape(batch * seq, d)

    def in_proj_args(layer):
        j = layer // N_MIXERS
        if layer % N_MIXERS == 0:
            return attn_w_in[j].astype(BF16), dict(
                scaled_cols=attn_w_out.shape[1], col_scale=ATTN_HEAD_DIM ** -0.5 * LOG2E)
        return lru_w_in[j].astype(BF16), {}

    w_in, in_kwargs = in_proj_args(0)
    proj = _norm_proj(x2d, pre_norm_w[0], w_in, **in_kwargs)
    for layer in range(depth):
        j = layer // N_MIXERS
        if layer + 1 < depth:
            w_in, in_kwargs = in_proj_args(layer + 1)
            next_args = (pre_norm_w[layer + 1], w_in)
        else:
            next_args, in_kwargs = (), {}
        if layer % N_MIXERS == 0:
            width = attn_w_out.shape[1]
            lam_init = 0.8 - 0.6 * math.exp(-0.3 * layer)
            y = _diff_attention(proj.reshape(batch, seq, 4 * width), attn_lambda_q1[j],
                                attn_lambda_k1[j], attn_lambda_q2[j], attn_lambda_k2[j],
                                attn_subln_w[j], lam_init)
            outs = _out_in_proj(y.reshape(batch * seq, width), attn_w_out[j].astype(BF16),
                                post_norm_w[layer], x2d, *next_args, **in_kwargs)
        else:
            width = lru_w_out.shape[1]
            outs = _rglru_proj(proj.reshape(batch, seq, 2 * width), lru_conv_w[j], lru_conv_b[j],
                               (lru_gate_a_w[j] * (-LOG2E)).astype(BF16), lru_gate_a_b[j],
                               (lru_gate_x_w[j] * (-LOG2E)).astype(BF16), lru_gate_x_b[j],
                               lru_log_a_param[j], lru_w_out[j].astype(BF16),
                               post_norm_w[layer], x2d, *next_args, **in_kwargs)
        x2d = outs[0]
        if len(outs) > 1:
            proj = outs[1]
    return x2d.reshape(batch, seq, d)
```

```python
import functools
import math

import jax
import jax.numpy as jnp
from jax import lax
from jax.experimental import pallas as pl
from jax.experimental.pallas import tpu as pltpu

F32 = jnp.float32
BF16 = jnp.bfloat16

N_MIXERS = 2
ATTN_HEADS = 8
ATTN_HEAD_DIM = 64
HEAD_WIDTH = 2 * ATTN_HEAD_DIM
LRU_BLOCKS = 4
CONV_WIDTH = 4
LRU_C = 8.0
NORM_EPS = 1e-6
LOG2E = math.log2(math.e)
MASKED_SCORE = -1e30

V7X_VMEM_BYTES = 64 * 1024 * 1024
V7X_SUBLANES = 8
V7X_LANES = 128

TOKEN_TILE = 1024
PROJ_N_CHUNK = 512
PROJ_ROW_PARTS = 2
ATTN_Q_TILE = 256
ATTN_KEY_SPAN = 512
ATTN_HEADS_PER_STEP = 2
ALIBI_PIECES = 3
LRU_TIME_TILE = 512
LRU_ROW_PARTS = 2


def _vmem_limit(block_bytes, temp_bytes):
    return int(min(V7X_VMEM_BYTES - (4 << 20), 2 * block_bytes + temp_bytes + (4 << 20)))


def _rms_scale(x):
    return lax.rsqrt(jnp.mean(x * x, axis=-1, keepdims=True) + NORM_EPS)


def _norm_proj_body(x, nw_ref, w_ref, o_ref, scaled_cols, col_scale):
    xn = ((x * _rms_scale(x)) * nw_ref[...]).astype(BF16)
    n_out = o_ref.shape[-1]
    for n0 in range(0, n_out, PROJ_N_CHUNK):
        acc = jnp.dot(xn, w_ref[:, n0:n0 + PROJ_N_CHUNK], preferred_element_type=F32)
        if n0 < scaled_cols:
            acc = acc * col_scale
        o_ref[:, n0:n0 + PROJ_N_CHUNK] = acc.astype(o_ref.dtype)


def _norm_proj_kernel(x_ref, nw_ref, w_ref, o_ref, *, scaled_cols, col_scale):
    part = x_ref.shape[0] // PROJ_ROW_PARTS
    for r0 in range(0, x_ref.shape[0], part):
        rows = slice(r0, r0 + part)
        _norm_proj_body(x_ref[rows, :], nw_ref, w_ref, o_ref.at[rows, :], scaled_cols, col_scale)


def _norm_proj(x2d, norm_w, w_bf16, *, scaled_cols=0, col_scale=1.0):
    tokens, d = x2d.shape
    n_out = w_bf16.shape[1]
    assert tokens % TOKEN_TILE == 0 and n_out % PROJ_N_CHUNK == 0
    assert scaled_cols % PROJ_N_CHUNK == 0
    blocks = TOKEN_TILE * d * 4 + d * 4 + d * n_out * 2 + TOKEN_TILE * n_out * 2
    temps = TOKEN_TILE * d * 6 + 2 * TOKEN_TILE * PROJ_N_CHUNK * 4
    return pl.pallas_call(
        functools.partial(_norm_proj_kernel, scaled_cols=scaled_cols, col_scale=col_scale),
        grid=(tokens // TOKEN_TILE,),
        in_specs=[
            pl.BlockSpec((TOKEN_TILE, d), lambda i: (i, 0)),
            pl.BlockSpec((1, d), lambda i: (0, 0)),
            pl.BlockSpec((d, n_out), lambda i: (0, 0)),
        ],
        out_specs=pl.BlockSpec((TOKEN_TILE, n_out), lambda i: (i, 0)),
        out_shape=jax.ShapeDtypeStruct((tokens, n_out), BF16),
        compiler_params=pltpu.CompilerParams(
            dimension_semantics=("arbitrary",),
            vmem_limit_bytes=_vmem_limit(blocks, temps)),
        name="norm_proj",
    )(x2d, norm_w.reshape(1, d), w_bf16)


def _out_in_proj_kernel(y_ref, wo_ref, pw_ref, x_ref, nw_ref, wi_ref, xo_ref, po_ref, *,
                        scaled_cols, col_scale):
    tm = x_ref.shape[0]
    half = tm // PROJ_ROW_PARTS
    starts = range(0, tm, half)
    outs = [jnp.dot(y_ref[r0:r0 + half, :], wo_ref[...], preferred_element_type=F32)
            for r0 in starts]
    for r0, out in zip(starts, outs):
        rows = slice(r0, r0 + half)
        x_new = x_ref[rows, :] + (out * _rms_scale(out)) * pw_ref[...]
        xo_ref[rows, :] = x_new
        _norm_proj_body(x_new, nw_ref, wi_ref, po_ref.at[rows, :], scaled_cols, col_scale)


def _out_in_proj(y2d, w_out_bf16, post_w, x2d, norm_w, w_in_bf16, *, scaled_cols=0,
                 col_scale=1.0):
    tokens, d = x2d.shape
    width = y2d.shape[1]
    n_out = w_in_bf16.shape[1]
    assert tokens % TOKEN_TILE == 0 and n_out % PROJ_N_CHUNK == 0
    assert scaled_cols % PROJ_N_CHUNK == 0
    blocks = (TOKEN_TILE * width * 2 + width * d * 2 + 2 * d * 4 + 2 * TOKEN_TILE * d * 4
              + d * n_out * 2 + TOKEN_TILE * n_out * 2)
    temps = TOKEN_TILE * d * 10 + 2 * TOKEN_TILE * PROJ_N_CHUNK * 4
    row = pl.BlockSpec((1, d), lambda i: (0, 0))
    return pl.pallas_call(
        functools.partial(_out_in_proj_kernel, scaled_cols=scaled_cols, col_scale=col_scale),
        grid=(tokens // TOKEN_TILE,),
        in_specs=[
            pl.BlockSpec((TOKEN_TILE, width), lambda i: (i, 0)),
            pl.BlockSpec((width, d), lambda i: (0, 0)),
            row,
            pl.BlockSpec((TOKEN_TILE, d), lambda i: (i, 0)),
            row,
            pl.BlockSpec((d, n_out), lambda i: (0, 0)),
        ],
        out_specs=[pl.BlockSpec((TOKEN_TILE, d), lambda i: (i, 0)),
                   pl.BlockSpec((TOKEN_TILE, n_out), lambda i: (i, 0))],
        out_shape=[jax.ShapeDtypeStruct((tokens, d), F32),
                   jax.ShapeDtypeStruct((tokens, n_out), BF16)],
        compiler_params=pltpu.CompilerParams(
            dimension_semantics=("arbitrary",),
            vmem_limit_bytes=_vmem_limit(blocks, temps)),
        name="out_in_proj",
    )(y2d, w_out_bf16, post_w.reshape(1, d), x2d, norm_w.reshape(1, d), w_in_bf16)


def _diff_attn_kernel(lq1_ref, lk1_ref, lq2_ref, lk2_ref, sw_ref, q_ref, k_ref, v_ref, g_ref,
                      o_ref, bias_ref, vt_ref, qs_ref, *, lam_init, n_heads):
    seq = q_ref.shape[0]
    tq = ATTN_Q_TILE
    n_tiles = seq // tq
    heads_here = q_ref.shape[1] // HEAD_WIDTH
    first_head = pl.program_id(1) * heads_here

    def head_lanes(hh):
        return slice(hh * HEAD_WIDTH, (hh + 1) * HEAD_WIDTH)

    @pl.when(pl.program_id(0) == 0)
    def _():
        for hh in range(heads_here):
            head_p1 = (first_head + hh + 1).astype(F32)
            slope = jnp.exp2(jnp.full((seq, HEAD_WIDTH), -(8.0 / n_heads), F32) * head_p1) * LOG2E
            key_pos = lax.broadcasted_iota(jnp.int32, (seq, HEAD_WIDTH), 0).astype(F32)
            key_lane = lax.broadcasted_iota(jnp.int32, (seq, HEAD_WIDTH), 1)
            bias = slope * key_pos
            bias_hi = bias.astype(BF16).astype(F32)
            rest = bias - bias_hi
            bias_mid = rest.astype(BF16).astype(F32)
            bias_lo = rest - bias_mid
            pieces = jnp.where(key_lane == 0, bias_hi,
                               jnp.where(key_lane == 1, bias_mid,
                                         jnp.where(key_lane == 2, bias_lo, 0.0)))
            bias_ref[first_head + hh] = pieces.astype(BF16)

    for hh in range(heads_here):
        vt_ref[hh, HEAD_WIDTH:, :] = jnp.ones((vt_ref.shape[1] - HEAD_WIDTH, seq), BF16)
    lam = (jnp.exp(jnp.sum(lq1_ref[...] * lk1_ref[...], keepdims=True))
           - jnp.exp(jnp.sum(lq2_ref[...] * lk2_ref[...], keepdims=True)) + lam_init)

    feature = lax.broadcasted_iota(jnp.int32, (HEAD_WIDTH, tq), 0)
    ones_rows = jnp.where(
        lax.broadcasted_iota(jnp.int32, (HEAD_WIDTH, 2 * tq), 0) < ALIBI_PIECES,
        1.0, 0.0).astype(BF16)
    out_scale = sw_ref[...] * (1.0 - lam_init)

    def stacked_queries(hh, qi):
        q_t = q_ref[qi * tq:(qi + 1) * tq, head_lanes(hh)].T
        zero = jnp.zeros_like(q_t)
        top = jnp.concatenate([jnp.where(feature < ATTN_HEAD_DIM, q_t, zero),
                               jnp.where(feature >= ATTN_HEAD_DIM, q_t, zero)], axis=1)
        return jnp.concatenate([top, ones_rows], axis=0)

    def scores(hh, q_stacked, keys_at):
        keys = jnp.concatenate([k_ref[keys_at, head_lanes(hh)],
                                bias_ref[first_head + hh, keys_at, :]], axis=1)
        return jnp.dot(keys, q_stacked, preferred_element_type=F32)

    chains = []
    for hh in range(heads_here):
        for qi in range(n_tiles):
            spans = []
            k0 = 0
            while k0 < qi * tq:
                k1 = min(k0 + ATTN_KEY_SPAN, qi * tq)
                spans.append((hh, qi, slice(k0, k1), k0 == 0, False))
                k0 = k1
            spans.append((hh, qi, slice(qi * tq, (qi + 1) * tq), qi == 0, True))
            chains.append(spans)
    tiles = [spans[r] for r in range(max(map(len, chains)))
             for spans in chains if r < len(spans)]

    def queries_for(hh, qi):
        if (hh, qi) not in stacked_done:
            qs_ref[hh, qi] = stacked_queries(hh, qi)
            stacked_done.add((hh, qi))
        return qs_ref[hh, qi]

    stacked_done = set()
    values_done = set()
    m_of = {}
    acc_of = {}
    s_next = scores(tiles[0][0], queries_for(tiles[0][0], tiles[0][1]), tiles[0][2])
    for t, (hh, qi, keys_at, first, diagonal) in enumerate(tiles):
        s = s_next
        if t + 1 < len(tiles):
            hh_n, qi_n, keys_n = tiles[t + 1][:3]
            s_next = scores(hh_n, queries_for(hh_n, qi_n), keys_n)
        if diagonal:
            key_row = lax.broadcasted_iota(jnp.int32, (tq, 2 * tq), 0)
            query_col = lax.broadcasted_iota(jnp.int32, (tq, 2 * tq), 1) & (tq - 1)
            s = jnp.where(key_row <= query_col, s, MASKED_SCORE)
        for k0 in range(keys_at.start, keys_at.stop, tq):
            if (hh, k0) not in values_done:
                vt_ref[hh, 0:HEAD_WIDTH, k0:k0 + tq] = v_ref[k0:k0 + tq, head_lanes(hh)].T
                values_done.add((hh, k0))
        values_t = vt_ref[hh, :, keys_at]
        chain = (hh, qi)
        if first:
            m = jnp.max(s, axis=0, keepdims=True)
            acc = jnp.dot(values_t, jnp.exp2(s - m).astype(BF16),
                          preferred_element_type=F32)
        else:
            m = jnp.maximum(m_of[chain], jnp.max(s, axis=0, keepdims=True))
            acc = jnp.exp2(m_of[chain] - m) * acc_of[chain] + jnp.dot(
                values_t, jnp.exp2(s - m).astype(BF16), preferred_element_type=F32)
        m_of[chain], acc_of[chain] = m, acc
        if not diagonal:
            continue

        num = acc[:HEAD_WIDTH]
        inv_den = 1.0 / acc[HEAD_WIDTH:HEAD_WIDTH + 1]
        o_t = num[:, :tq] * inv_den[:, :tq] - num[:, tq:] * (lam * inv_den[:, tq:])
        o_t = o_t * lax.rsqrt(jnp.mean(o_t * o_t, axis=0, keepdims=True) + NORM_EPS)
        g = g_ref[qi * tq:(qi + 1) * tq, head_lanes(hh)].astype(F32)
        silu_g = g * _sigmoid_of_scaled(g * (-LOG2E))
        o_ref[qi * tq:(qi + 1) * tq, head_lanes(hh)] = (
            (o_t.T * out_scale) * silu_g).astype(o_ref.dtype)


def _diff_attention(proj, lq1, lk1, lq2, lk2, subln_w, lam_init):
    batch, seq, n_proj = proj.shape
    width = n_proj // 4
    n_heads = width // HEAD_WIDTH
    assert seq % ATTN_Q_TILE == 0
    d = lq1.shape[-1]

    hps = ATTN_HEADS_PER_STEP
    assert n_heads % hps == 0
    groups = n_heads // hps

    def head_block(section):
        return pl.BlockSpec((None, seq, hps * HEAD_WIDTH),
                            lambda b, h, s=section: (b, 0, s * groups + h))

    def small(n):
        return pl.BlockSpec((1, n), lambda b, h: (0, 0))

    blocks = 5 * seq * hps * HEAD_WIDTH * 2
    scratch_bytes = ((n_heads + hps) * seq * HEAD_WIDTH * 2 + hps * 2 * V7X_SUBLANES * seq * 2
                     + hps * (seq // ATTN_Q_TILE) * 2 * HEAD_WIDTH * 2 * ATTN_Q_TILE * 2)
    temps = scratch_bytes + hps * 24 * (2 * ATTN_Q_TILE) * ATTN_Q_TILE * 4
    return pl.pallas_call(
        functools.partial(_diff_attn_kernel, lam_init=lam_init, n_heads=n_heads),
        grid=(batch, groups),
        in_specs=[small(d), small(d), small(d), small(d), small(HEAD_WIDTH),
                  head_block(0), head_block(1), head_block(2), head_block(3)],
        out_specs=pl.BlockSpec((None, seq, hps * HEAD_WIDTH), lambda b, h: (b, 0, h)),
        out_shape=jax.ShapeDtypeStruct((batch, seq, width), BF16),
        scratch_shapes=[
            pltpu.VMEM((n_heads, seq, HEAD_WIDTH), BF16),
            pltpu.VMEM((hps, HEAD_WIDTH + 2 * V7X_SUBLANES, seq), BF16),
            pltpu.VMEM((hps, seq // ATTN_Q_TILE, 2 * HEAD_WIDTH, 2 * ATTN_Q_TILE), BF16),
        ],
        compiler_params=pltpu.CompilerParams(
            dimension_semantics=("arbitrary", "arbitrary"),
            vmem_limit_bytes=_vmem_limit(blocks, temps)),
        name="diff_attn",
    )(lq1.reshape(1, d), lk1.reshape(1, d), lq2.reshape(1, d), lk2.reshape(1, d),
      subln_w.reshape(1, HEAD_WIDTH), proj, proj, proj, proj)


def _sigmoid_of_scaled(x_scaled):
    return 1.0 / (1.0 + jnp.exp2(x_scaled))


def _one_minus_square(log2_a):
    t = jnp.tanh(log2_a * (1.0 / LOG2E))
    return (-2.0 * t) / (1.0 - t)


def _sqrt_nonneg(x):
    positive = x > 0.0
    return jnp.where(positive, x * lax.rsqrt(jnp.where(positive, x, 1.0)), 0.0)


def _rglru_rows_kernel(xr_ref, g_ref, cw_ref, cb_ref, wa_ref, ba_ref, wx_ref, bx_ref, la_ref,
                       wo_ref, pw_ref, x_ref, *rest, has_next, scaled_cols, col_scale):
    if has_next:
        nw_ref, wi_ref, xo_ref, po_ref = rest[:4]
    else:
        xo_ref = rest[0]
    halo_ref, hc_ref, xc_ref, za_ref, zx_ref, h_ref = rest[-6:]
    tt, c = xr_ref.shape
    sub = V7X_SUBLANES
    blk = c // LRU_BLOCKS

    @pl.when(pl.program_id(1) == 0)
    def _():
        halo_ref[...] = jnp.zeros_like(halo_ref)
        hc_ref[...] = jnp.zeros_like(hc_ref)

    row = lax.broadcasted_iota(jnp.int32, (sub, c), 0)

    x_all = xr_ref[...].astype(F32)
    taps = [cw_ref[j:j + 1, :] for j in range(CONV_WIDTH)]
    conv_b = cb_ref[...]
    delays = range(1, CONV_WIDTH)
    prev_rolled = [pltpu.roll(halo_ref[...], d, 0) for d in delays]
    for g0 in range(0, tt, sub):
        x = x_all[g0:g0 + sub]
        rolled = [pltpu.roll(x, d, 0) for d in delays]
        xc = conv_b + taps[CONV_WIDTH - 1] * x
        for d in delays:
            delayed = jnp.where(row >= d, rolled[d - 1], prev_rolled[d - 1])
            xc = xc + taps[CONV_WIDTH - 1 - d] * delayed
        xc_ref[g0:g0 + sub, :] = xc
        prev_rolled = rolled
    halo_ref[...] = x_all[tt - sub:tt]

    xc_bf = xc_ref[...].astype(BF16)
    for blk_i in range(LRU_BLOCKS):
        cols = slice(blk_i * blk, (blk_i + 1) * blk)
        za_ref[:, cols] = jnp.dot(xc_bf[:, cols], wa_ref[blk_i], preferred_element_type=F32)
        zx_ref[:, cols] = jnp.dot(xc_bf[:, cols], wx_ref[blk_i], preferred_element_type=F32)

    z = -la_ref[...]
    softplus = jnp.maximum(z, 0.0) + jnp.log1p(jnp.exp(-jnp.abs(z)))
    log2_a_scale = (-LRU_C * LOG2E) * softplus
    gate_a_b = ba_ref[...] * (-LOG2E)
    gate_x_b = bx_ref[...] * (-LOG2E)

    shifts = [s for s in (1, 2, 4) if s < sub]
    valid = [row >= s for s in shifts]
    h_last = hc_ref[...]
    part = tt // LRU_ROW_PARTS
    for p0 in range(0, tt, part):
        for g0 in range(p0, p0 + part, sub):
            rows = slice(g0, g0 + sub)
            xc = xc_ref[rows, :]
            r = _sigmoid_of_scaled(za_ref[rows, :] + gate_a_b)
            i = _sigmoid_of_scaled(zx_ref[rows, :] + gate_x_b)
            log2_a = log2_a_scale * r
            a = jnp.exp2(log2_a)
            b = _sqrt_nonneg(_one_minus_square(log2_a)) * (i * xc)
            for s, ok in zip(shifts, valid):
                a_prev = jnp.where(ok, pltpu.roll(a, s, 0), 1.0)
                b_prev = jnp.where(ok, pltpu.roll(b, s, 0), 0.0)
                b = a * b_prev + b
                a = a * a_prev
            h = a * h_last + b
            h_ref[rows, :] = h
            h_last = h[sub - 1:sub]

        rows = slice(p0, p0 + part)
        g = g_ref[rows, :].astype(F32)
        y = (h_ref[rows, :] * (g * _sigmoid_of_scaled(g * (-LOG2E)))).astype(BF16)
        out = jnp.dot(y, wo_ref[...], preferred_element_type=F32)
        x_new = x_ref[rows, :] + (out * _rms_scale(out)) * pw_ref[...]
        xo_ref[rows, :] = x_new
        if has_next:
            _norm_proj_body(x_new, nw_ref, wi_ref, po_ref.at[rows, :], scaled_cols, col_scale)
    hc_ref[...] = h_last


def _rglru_rows(proj, conv_w, conv_b, wa_bf16, ba, wx_bf16, bx, log_a_param, w_out_bf16, post_w,
                x2d, norm_w=None, w_in_bf16=None, *, scaled_cols=0, col_scale=1.0):
    batch, seq, n_proj = proj.shape
    c = n_proj // 2
    tokens, d = x2d.shape
    tt = LRU_TIME_TILE
    blk = c // LRU_BLOCKS
    assert seq % tt == 0 and tt % (LRU_ROW_PARTS * 2 * V7X_SUBLANES) == 0
    tiles_per_row = seq // tt
    has_next = w_in_bf16 is not None

    def lru_tile(col):
        return pl.BlockSpec((None, tt, c), lambda b, t: (b, t, col))

    def token_tile(width):
        return pl.BlockSpec((tt, width), lambda b, t: (b * tiles_per_row + t, 0))

    def whole(shape):
        return pl.BlockSpec(shape, lambda b, t: (0,) * len(shape))

    in_specs = [lru_tile(0), lru_tile(1), whole((CONV_WIDTH, c)), whole((1, c)),
                whole((LRU_BLOCKS, blk, blk)), whole((1, c)), whole((LRU_BLOCKS, blk, blk)),
                whole((1, c)), whole((1, c)), whole((c, d)), whole((1, d)), token_tile(d)]
    operands = [proj, proj, conv_w, conv_b.reshape(1, c), wa_bf16, ba.reshape(1, c), wx_bf16,
                bx.reshape(1, c), log_a_param.reshape(1, c), w_out_bf16, post_w.reshape(1, d),
                x2d]
    out_specs = [token_tile(d)]
    out_shape = [jax.ShapeDtypeStruct((tokens, d), F32)]
    blocks = (2 * tt * c * 2 + 2 * LRU_BLOCKS * blk * blk * 2 + c * d * 2 + 2 * tt * d * 4
              + 8 * c * 4)
    if has_next:
        n_out = w_in_bf16.shape[1]
        assert n_out % PROJ_N_CHUNK == 0 and scaled_cols % PROJ_N_CHUNK == 0
        in_specs += [whole((1, d)), whole((d, n_out))]
        operands += [norm_w.reshape(1, d), w_in_bf16]
        out_specs.append(token_tile(n_out))
        out_shape.append(jax.ShapeDtypeStruct((tokens, n_out), BF16))
        blocks += d * n_out * 2 + tt * n_out * 2
    scratch_bytes = 4 * tt * c * 4
    temps = scratch_bytes + 6 * tt * d * 4
    return pl.pallas_call(
        functools.partial(_rglru_rows_kernel, has_next=has_next, scaled_cols=scaled_cols,
                          col_scale=col_scale),
        grid=(batch, tiles_per_row),
        in_specs=in_specs,
        out_specs=out_specs,
        out_shape=out_shape,
        scratch_shapes=[
            pltpu.VMEM((V7X_SUBLANES, c), F32),
            pltpu.VMEM((1, c), F32),
            pltpu.VMEM((tt, c), F32),
            pltpu.VMEM((tt, c), F32),
            pltpu.VMEM((tt, c), F32),
            pltpu.VMEM((tt, c), F32),
        ],
        compiler_params=pltpu.CompilerParams(
            dimension_semantics=("arbitrary", "arbitrary"),
            vmem_limit_bytes=_vmem_limit(blocks, temps)),
        name="rglru_rows",
    )(*operands)


def kernel(x, pre_norm_w, post_norm_w, attn_w_in, attn_w_out, attn_lambda_q1, attn_lambda_k1,
           attn_lambda_q2, attn_lambda_k2, attn_subln_w, lru_w_in, lru_conv_w, lru_conv_b,
           lru_gate_a_w, lru_gate_a_b, lru_gate_x_w, lru_gate_x_b, lru_log_a_param, lru_w_out):
    batch, seq, d = x.shape
    depth = pre_norm_w.shape[0]
    assert depth % N_MIXERS == 0, "the last layer must be an RG-LRU layer"
    x2d = x.reshape(batch * seq, d)

    def in_proj_args(layer):
        j = layer // N_MIXERS
        if layer % N_MIXERS == 0:
            return attn_w_in[j].astype(BF16), dict(
                scaled_cols=attn_w_out.shape[1], col_scale=ATTN_HEAD_DIM ** -0.5 * LOG2E)
        return lru_w_in[j].astype(BF16), {}

    w_in, in_kwargs = in_proj_args(0)
    proj = _norm_proj(x2d, pre_norm_w[0], w_in, **in_kwargs)
    for layer in range(depth):
        j = layer // N_MIXERS
        if layer + 1 < depth:
            w_in, in_kwargs = in_proj_args(layer + 1)
            next_args = (pre_norm_w[layer + 1], w_in)
        else:
            next_args, in_kwargs = (), {}
        if layer % N_MIXERS == 0:
            width = attn_w_out.shape[1]
            lam_init = 0.8 - 0.6 * math.exp(-0.3 * layer)
            y = _diff_attention(proj.reshape(batch, seq, 4 * width), attn_lambda_q1[j],
                                attn_lambda_k1[j], attn_lambda_q2[j], attn_lambda_k2[j],
                                attn_subln_w[j], lam_init)
            outs = _out_in_proj(y.reshape(batch * seq, width), attn_w_out[j].astype(BF16),
                                post_norm_w[layer], x2d, *next_args, **in_kwargs)
        else:
            width = lru_w_out.shape[1]
            outs = _rglru_rows(proj.reshape(batch, seq, 2 * width), lru_conv_w[j], lru_conv_b[j],
                               (lru_gate_a_w[j] * (-LOG2E)).astype(BF16), lru_gate_a_b[j],
                               (lru_gate_x_w[j] * (-LOG2E)).astype(BF16), lru_gate_x_b[j],
                               lru_log_a_param[j], lru_w_out[j].astype(BF16),
                               post_norm_w[layer], x2d, *next_args, **in_kwargs)
        x2d = outs[0]
        if len(outs) > 1:
            proj = outs[1]
    return x2d.reshape(batch, seq, d)
```
